```python
import math
import jax, jax.numpy as jnp
from jax import lax
import numpy as np

D_MODEL = 2048
BATCH = 4
SEQ = 2048
DEPTH = 2
DEC_BATCH = 128
DEC_SEQ = 8
PAST_LEN = 2048
PAGE_SIZE = 128

N_MIXERS = 2
N_GDN_LAYERS = (DEPTH + 1) // 2
N_SWA_LAYERS = DEPTH // 2

GDN_QK_HEADS = 16
GDN_V_HEADS = 32
GDN_DK = 128
GDN_DV = 128
GDN_CONV = 4
GDN_CHUNK = 64
GDN_QK_DIM = GDN_QK_HEADS * GDN_DK
GDN_V_DIM = GDN_V_HEADS * GDN_DV
GDN_CONV_DIM = 2 * GDN_QK_DIM + GDN_V_DIM
GDN_IN_DIM = GDN_CONV_DIM + GDN_V_DIM + 2 * GDN_V_HEADS

SWA_GROUPS = ((128, 1), (512, 4), (2048, 16))
SWA_N_GROUPS = 3
SWA_HPG = 6
SWA_HEAD_DIM = 128
SWA_INNER = SWA_N_GROUPS * SWA_HPG * SWA_HEAD_DIM
SWA_BLOCK = 128
ROPE_THETA = 10000.0

MEM_LEN = 256
MEM_HEADS = 4
MEM_HEAD_DIM = 128
MEM_INNER = MEM_HEADS * MEM_HEAD_DIM

FFN_DIM = 5632
FFN_CONV = 3
NORM_EPS = 1e-6

kernel_name = "hybrid_gdn_dilated_swa_memxattn_convffn_step"


def rms_norm(x, g):
    xf = x.astype(jnp.float32)
    y = xf * lax.rsqrt(jnp.mean(xf * xf, axis=-1, keepdims=True) + NORM_EPS)
    return (y * g.astype(jnp.float32)).astype(x.dtype)


def l2norm(x):
    xf = x.astype(jnp.float32)
    return xf * lax.rsqrt(jnp.sum(xf * xf, axis=-1, keepdims=True) + NORM_EPS)


def rope(x, pos):
    half = x.shape[-1] // 2
    inv = ROPE_THETA ** (-jnp.arange(half, dtype=jnp.float32) / half)
    ang = pos.astype(jnp.float32)[:, None] * inv[None, :]
    cos = jnp.cos(ang)[:, None, :]
    sin = jnp.sin(ang)[:, None, :]
    xf = x.astype(jnp.float32)
    x1, x2 = xf[..., :half], xf[..., half:]
    return jnp.concatenate([x1 * cos - x2 * sin, x2 * cos + x1 * sin], axis=-1).astype(x.dtype)


def causal_dwconv(x, buf, w):
    T = x.shape[1]
    width = w.shape[0]
    xp = jnp.concatenate([buf.astype(x.dtype), x], axis=1)
    y = xp[:, :T] * w[0]
    for i in range(1, width):
        y = y + xp[:, i:i + T] * w[i]
    return y, xp[:, T:]


def chunk_gated_delta_rule(q, k, v, beta, g, state):
    B, T, H, dk = q.shape
    dv = v.shape[-1]
    C = min(GDN_CHUNK, T)
    N = -(-T // C)
    pad = N * C - T

    def chunk4(x):
        x = jnp.pad(x, ((0, 0), (0, pad), (0, 0), (0, 0)))
        return x.reshape(B, N, C, H, x.shape[-1]).transpose(1, 0, 3, 2, 4)

    def chunk3(x):
        x = jnp.pad(x, ((0, 0), (0, pad), (0, 0)))
        return x.reshape(B, N, C, H).transpose(1, 0, 3, 2)

    qc, kc, vc = chunk4(q), chunk4(k), chunk4(v)
    bc = chunk3(beta)
    gc = jnp.cumsum(chunk3(g), axis=-1)
    causal = jnp.tril(jnp.ones((C, C), dtype=bool))
    strict = jnp.tril(jnp.ones((C, C), dtype=bool), -1)
    diff = gc[..., :, None] - gc[..., None, :]
    decay = jnp.where(causal, jnp.exp(jnp.where(causal, diff, 0.0)), 0.0)
    kb = kc * bc[..., None]
    lmat = jnp.where(strict, jnp.einsum('nbhid,nbhjd->nbhij', kb, kc) * decay, 0.0)
    eye = jnp.broadcast_to(jnp.eye(C, dtype=jnp.float32), lmat.shape)
    tinv = lax.linalg.triangular_solve(lmat + eye, eye, left_side=True, lower=True, unit_diagonal=True)
    u = tinv @ (vc * bc[..., None])
    w = tinv @ (kb * jnp.exp(gc)[..., None])
    a_intra = jnp.where(causal, jnp.einsum('nbhid,nbhjd->nbhij', qc, kc) * decay, 0.0)
    g_last = gc[..., -1]

    def step(S, xs):
        q_i, k_i, u_i, w_i, a_i, g_i, gl = xs
        v_new = u_i - w_i @ S
        o = (q_i * jnp.exp(g_i)[..., None]) @ S + a_i @ v_new
        k_dec = k_i * jnp.exp(gl[..., None] - g_i)[..., None]
        S = S * jnp.exp(gl)[..., None, None] + jnp.einsum('bhcd,bhce->bhde', k_dec, v_new)
        return S, o

    S, o = lax.scan(step, state, (qc, kc, u, w, a_intra, gc, g_last))
    o = o.transpose(1, 0, 3, 2, 4).reshape(B, N * C, H, dv)[:, :T]
    return o, S


def gdn_mixer(h, conv_buf, state, w_in, conv_w, a_log, dt_bias, o_gain, w_out):
    B, T, _ = h.shape
    proj = h @ w_in
    c1 = GDN_CONV_DIM
    c2 = c1 + GDN_V_DIM
    c3 = c2 + GDN_V_HEADS
    qkv, z, b, a = proj[..., :c1], proj[..., c1:c2], proj[..., c2:c3], proj[..., c3:]
    qkv, new_buf = causal_dwconv(qkv, conv_buf, conv_w)
    qkv = jax.nn.silu(qkv)
    rep = GDN_V_HEADS // GDN_QK_HEADS
    q = l2norm(qkv[..., :GDN_QK_DIM].reshape(B, T, GDN_QK_HEADS, GDN_DK)) * (GDN_DK ** -0.5)
    k = l2norm(qkv[..., GDN_QK_DIM:2 * GDN_QK_DIM].reshape(B, T, GDN_QK_HEADS, GDN_DK))
    q = jnp.repeat(q, rep, axis=2)
    k = jnp.repeat(k, rep, axis=2)
    v = qkv[..., 2 * GDN_QK_DIM:].reshape(B, T, GDN_V_HEADS, GDN_DV).astype(jnp.float32)
    beta = jax.nn.sigmoid(b.astype(jnp.float32))
    g = -jnp.exp(a_log.astype(jnp.float32)) * jax.nn.softplus(a.astype(jnp.float32) + dt_bias.astype(jnp.float32))
    o, new_state = chunk_gated_delta_rule(q, k, v, beta, g, state.astype(jnp.float32))
    zf = z.reshape(B, T, GDN_V_HEADS, GDN_DV).astype(jnp.float32)
    o = rms_norm(o, o_gain) * jax.nn.silu(zf)
    y = o.astype(h.dtype).reshape(B, T, GDN_V_DIM) @ w_out
    return y, new_buf, new_state.astype(state.dtype)


def banded_attention(q, k, v, span):
    N, L, H, hd = q.shape
    Bq = SWA_BLOCK
    nb = -(-L // Bq)
    Lp = nb * Bq
    qb = jnp.pad(q, ((0, 0), (0, Lp - L), (0, 0), (0, 0))).reshape(N, nb, Bq, H, hd)

    def key_blocks(x):
        xp = jnp.pad(x, ((0, 0), (Bq, Lp - L), (0, 0), (0, 0))).reshape(N, nb + 1, Bq, H, hd)
        return jnp.concatenate([xp[:, :-1], xp[:, 1:]], axis=2)

    kb = key_blocks(k).astype(jnp.float32)
    vb = key_blocks(v).astype(jnp.float32)
    s = jnp.einsum('nbqhd,nbkhd->nbhqk', qb.astype(jnp.float32), kb) * (hd ** -0.5)
    i = jnp.arange(Bq)[:, None]
    j = jnp.arange(2 * Bq)[None, :]
    dist = i + Bq - j
    kpos = jnp.arange(nb)[:, None, None] * Bq - Bq + j[None]
    valid = (dist >= 0)[None] & (dist <= span)[None] & (kpos >= 0)
    s = jnp.where(valid[None, :, None], s, -jnp.inf)
    lse = jax.nn.logsumexp(s, axis=-1)
    p = jnp.exp(s - lse[..., None])
    o = jnp.einsum('nbhqk,nbkhd->nbqhd', p, vb)
    o = o.reshape(N, Lp, H, hd)[:, :L].astype(q.dtype)
    lse = lse.transpose(0, 1, 3, 2).reshape(N, Lp, H)[:, :L]
    return o, lse


def dilated_window_prompt(q, k, v, window, dil):
    B, T, H, hd = q.shape
    L = T // dil

    def split(x):
        return x.reshape(B, L, dil, H, hd).transpose(0, 2, 1, 3, 4).reshape(B * dil, L, H, hd)

    o, lse = banded_attention(split(q), split(k), split(v), window // dil)
    o = o.reshape(B, dil, L, H, hd).transpose(0, 2, 1, 3, 4).reshape(B, T, H, hd)
    lse = lse.reshape(B, dil, L, H).transpose(0, 2, 1, 3).reshape(B, T, H)
    return o, lse


def dilated_window_decode(q, k, v, buf, window, dil):
    DB, S, H, hd = q.shape
    Lb = buf.shape[1]
    kv_all = jnp.concatenate([buf, jnp.stack([k, v], axis=2).astype(buf.dtype)], axis=1)
    J = window // dil + 1
    idx = Lb + jnp.arange(S)[:, None] - dil * jnp.arange(J)[None, :]
    valid = idx >= 0
    kv_g = jnp.take(kv_all, jnp.maximum(idx, 0), axis=1)
    s = jnp.einsum('bshd,bsjhd->bhsj', q.astype(jnp.float32), kv_g[:, :, :, 0].astype(jnp.float32)) * (hd ** -0.5)
    s = jnp.where(valid[None, None], s, -jnp.inf)
    lse = jax.nn.logsumexp(s, axis=-1)
    p = jnp.exp(s - lse[..., None])
    o = jnp.einsum('bhsj,bsjhd->bshd', p, kv_g[:, :, :, 1].astype(jnp.float32)).astype(q.dtype)
    return o, lse.transpose(0, 2, 1), kv_all[:, S:]


def swa_mixer(h, pos, bufs, w_qkv, w_out):
    B, T, _ = h.shape
    n_heads = SWA_N_GROUPS * SWA_HPG
    qkv = (h @ w_qkv).reshape(B, T, 3, n_heads, SWA_HEAD_DIM)
    gshape = (B, T, SWA_N_GROUPS, SWA_HPG, SWA_HEAD_DIM)
    q = rope(qkv[:, :, 0], pos).reshape(gshape)
    k = rope(qkv[:, :, 1], pos).reshape(gshape)
    v = qkv[:, :, 2].reshape(gshape)
    outs, lses, new_bufs = [], [], []
    for gi, (window, dil) in enumerate(SWA_GROUPS):
        qg, kg, vg = q[:, :, gi], k[:, :, gi], v[:, :, gi]
        if bufs is None:
            o, lse = dilated_window_prompt(qg, kg, vg, window, dil)
            nb = jnp.stack([kg, vg], axis=2)[:, -min(window, T):]
        else:
            o, lse, nb = dilated_window_decode(qg, kg, vg, bufs[gi], window, dil)
        outs.append(o)
        lses.append(lse)
        new_bufs.append(nb)
    alpha = jax.nn.softmax(jnp.stack(lses, axis=2), axis=2)
    o = (jnp.stack(outs, axis=2).astype(jnp.float32) * alpha[..., None]).astype(h.dtype)
    return o.reshape(B, T, SWA_INNER) @ w_out, new_bufs


def mem_kv_proj(mem, w_kv):
    B, M, _ = mem.shape
    return (mem @ w_kv).reshape(B, M, 2, MEM_HEADS, MEM_HEAD_DIM)


def mem_cross_attn(h, mem_kv, w_q, w_o):
    B, T, _ = h.shape
    q = (h @ w_q).reshape(B, T, MEM_HEADS, MEM_HEAD_DIM)
    s = jnp.einsum('bthd,bmhd->bhtm', q.astype(jnp.float32), mem_kv[:, :, 0].astype(jnp.float32)) * (MEM_HEAD_DIM ** -0.5)
    p = jax.nn.softmax(s, axis=-1)
    o = jnp.einsum('bhtm,bmhd->bthd', p, mem_kv[:, :, 1].astype(jnp.float32)).astype(h.dtype)
    return o.reshape(B, T, MEM_INNER) @ w_o


def conv_ffn(h, buf, w_up, conv_w, conv_b, w_down):
    gu = h @ w_up
    gate, up = gu[..., :FFN_DIM], gu[..., FFN_DIM:]
    gate, new_buf = causal_dwconv(gate, buf, conv_w)
    return (jax.nn.gelu(gate + conv_b) * up) @ w_down, new_buf


def setup_inputs(seed: int = 0) -> dict:
    key = jax.random.key(seed)
    keys = iter(jax.random.split(key, 40))

    def nrm(shape, scale=1.0):
        return jax.random.normal(next(keys), shape, jnp.float32) * scale

    n_a, n_b = N_GDN_LAYERS, N_SWA_LAYERS
    swa_lens = [min(w, PAST_LEN) for w, _ in SWA_GROUPS]
    a_log = jnp.log(jax.random.uniform(next(keys), (n_a, GDN_V_HEADS), jnp.float32, 1.0, 16.0))
    dt = jnp.exp(jax.random.uniform(next(keys), (n_a, GDN_V_HEADS), jnp.float32, math.log(1e-3), math.log(1e-1)))
    dt_bias = dt + jnp.log(-jnp.expm1(-dt))
    return {
        "x_prompt": nrm((BATCH, SEQ, D_MODEL)),
        "x_sample": nrm((DEC_BATCH, DEC_SEQ, D_MODEL)),
        "state_gdn": nrm((n_a, DEC_BATCH, GDN_V_HEADS, GDN_DK, GDN_DV), 0.1),
        "state_gdn_conv": nrm((n_a, DEC_BATCH, GDN_CONV - 1, GDN_CONV_DIM)),
        "cache_swa_kv0": nrm((n_b, DEC_BATCH, swa_lens[0], 2, SWA_HPG, SWA_HEAD_DIM)),
        "cache_swa_kv1": nrm((n_b, DEC_BATCH, swa_lens[1], 2, SWA_HPG, SWA_HEAD_DIM)),
        "cache_swa_kv2": nrm((n_b, DEC_BATCH, swa_lens[2], 2, SWA_HPG, SWA_HEAD_DIM)),
        "cache_mem_kv": nrm((DEPTH, DEC_BATCH, MEM_LEN, 2, MEM_HEADS, MEM_HEAD_DIM)),
        "state_ffn_conv": nrm((DEPTH, DEC_BATCH, FFN_CONV - 1, FFN_DIM)),
        "mem_prompt": nrm((BATCH, MEM_LEN, D_MODEL)),
        "norm_gains": 1.0 + nrm((DEPTH, 6, D_MODEL), 0.05),
        "gdn_w_in": nrm((n_a, D_MODEL, GDN_IN_DIM), D_MODEL ** -0.5),
        "gdn_conv_w": nrm((n_a, GDN_CONV, GDN_CONV_DIM), GDN_CONV ** -0.5),
        "gdn_a_log": a_log,
        "gdn_dt_bias": dt_bias,
        "gdn_norm": 1.0 + nrm((n_a, GDN_DV), 0.05),
        "gdn_w_out": nrm((n_a, GDN_V_DIM, D_MODEL), GDN_V_DIM ** -0.5),
        "swa_w_qkv": nrm((n_b, D_MODEL, 3 * SWA_INNER), D_MODEL ** -0.5),
        "swa_w_out": nrm((n_b, SWA_INNER, D_MODEL), SWA_INNER ** -0.5),
        "mem_w_q": nrm((DEPTH, D_MODEL, MEM_INNER), D_MODEL ** -0.5),
        "mem_w_kv": nrm((DEPTH, D_MODEL, 2 * MEM_INNER), D_MODEL ** -0.5),
        "mem_w_o": nrm((DEPTH, MEM_INNER, D_MODEL), MEM_INNER ** -0.5),
        "ffn_w_up": nrm((DEPTH, D_MODEL, 2 * FFN_DIM), D_MODEL ** -0.5),
        "ffn_conv_w": nrm((DEPTH, FFN_CONV, FFN_DIM), FFN_CONV ** -0.5),
        "ffn_conv_b": nrm((DEPTH, FFN_DIM), 0.02),
        "ffn_w_down": nrm((DEPTH, FFN_DIM, D_MODEL), FFN_DIM ** -0.5),
    }


def reference(x_prompt, x_sample, state_gdn, state_gdn_conv, cache_swa_kv0, cache_swa_kv1, cache_swa_kv2,
              cache_mem_kv, state_ffn_conv, mem_prompt, norm_gains, gdn_w_in, gdn_conv_w, gdn_a_log,
              gdn_dt_bias, gdn_norm, gdn_w_out, swa_w_qkv, swa_w_out, mem_w_q, mem_w_kv, mem_w_o,
              ffn_w_up, ffn_conv_w, ffn_conv_b, ffn_w_down):

    def run_trunk(x, pos, mem_kv, prompt, gdn_s, gdn_c, swa_c, ffn_c):
        B = x.shape[0]
        out_gdn_s, out_gdn_c, out_ffn = [], [], []
        out_swa = [[] for _ in SWA_GROUPS]
        ia = 0
        ib = 0
        for layer in range(DEPTH):
            g = norm_gains[layer]
            h = rms_norm(x, g[0])
            if layer % N_MIXERS == 0:
                if prompt:
                    s0 = jnp.zeros((B, GDN_V_HEADS, GDN_DK, GDN_DV), x.dtype)
                    c0 = jnp.zeros((B, GDN_CONV - 1, GDN_CONV_DIM), x.dtype)
                else:
                    s0, c0 = gdn_s[ia], gdn_c[ia]
                y, c1, s1 = gdn_mixer(h, c0, s0, gdn_w_in[ia], gdn_conv_w[ia], gdn_a_log[ia],
                                      gdn_dt_bias[ia], gdn_norm[ia], gdn_w_out[ia])
                out_gdn_s.append(s1)
                out_gdn_c.append(c1)
                ia += 1
            else:
                bufs = None if prompt else [c[ib] for c in swa_c]
                y, nbufs = swa_mixer(h, pos, bufs, swa_w_qkv[ib], swa_w_out[ib])
                for lst, nb in zip(out_swa, nbufs):
                    lst.append(nb)
                ib += 1
            x = x + rms_norm(y, g[1])
            h = rms_norm(x, g[2])
            x = x + rms_norm(mem_cross_attn(h, mem_kv[layer], mem_w_q[layer], mem_w_o[layer]), g[3])
            h = rms_norm(x, g[4])
            f0 = jnp.zeros((B, FFN_CONV - 1, FFN_DIM), x.dtype) if prompt else ffn_c[layer]
            y, f1 = conv_ffn(h, f0, ffn_w_up[layer], ffn_conv_w[layer], ffn_conv_b[layer], ffn_w_down[layer])
            out_ffn.append(f1)
            x = x + rms_norm(y, g[5])
        swa_states = [jnp.stack(lst) for lst in out_swa]
        return x, jnp.stack(out_gdn_s), jnp.stack(out_gdn_c), swa_states, jnp.stack(out_ffn)

    mem_kv_p = jnp.stack([mem_kv_proj(mem_prompt, mem_w_kv[l]) for l in range(DEPTH)])
    pos_p = jnp.arange(x_prompt.shape[1])
    y_prompt, p_gdn_s, p_gdn_c, p_swa, p_ffn = run_trunk(
        x_prompt, pos_p, mem_kv_p, True, None, None, None, None)

    pos_s = PAST_LEN + jnp.arange(x_sample.shape[1])
    y_sample, s_gdn_s, s_gdn_c, s_swa, s_ffn = run_trunk(
        x_sample, pos_s, cache_mem_kv, False, state_gdn, state_gdn_conv,
        (cache_swa_kv0, cache_swa_kv1, cache_swa_kv2), state_ffn_conv)

    return (y_prompt, y_sample, p_gdn_s, p_gdn_c, p_swa[0], p_swa[1], p_swa[2], mem_kv_p, p_ffn,
            s_gdn_s, s_gdn_c, s_swa[0], s_swa[1], s_swa[2], s_ffn)
```

```python
import functools
import math

import jax
import jax.numpy as jnp
from jax import lax
from jax.experimental import pallas as pl
from jax.experimental.pallas import tpu as pltpu

F32 = jnp.float32
BF16 = jnp.bfloat16
HIGHEST = lax.Precision.HIGHEST

LANE = 128
SUBLANE = 8
NORM_EPS = 1e-6
ROPE_THETA = 10000.0
PAST_LEN = 2048
GDN_CHUNK = 64
GDN_QK_HEADS = 16
GDN_V_HEADS = 32
GDN_CONV = 4
SWA_GROUPS = ((128, 1), (512, 4), (2048, 16))
SWA_HPG = 6
SWA_SPAN = 128
MEM_HEADS = 4
FFN_CONV = 3
VMEM_LIMIT_BYTES = 52 * 1024 * 1024


def _params(*semantics):
    return pltpu.CompilerParams(dimension_semantics=semantics, vmem_limit_bytes=VMEM_LIMIT_BYTES)


def _dot(a, b):
    return jnp.dot(a, b, preferred_element_type=F32)


def _dot_nt(a, b):
    return lax.dot_general(a, b, (((1,), (1,)), ((), ())), preferred_element_type=F32)


def _dot_tn(a, b):
    return lax.dot_general(a, b, (((0,), (0,)), ((), ())), preferred_element_type=F32)


def _rms(x, gain):
    return x * lax.rsqrt(jnp.mean(x * x, axis=-1, keepdims=True) + NORM_EPS) * gain


def _silu(x):
    return x * jax.nn.sigmoid(x)


def _nmm_kernel(x_ref, g_ref, w_ref, o_ref, h_ref, *, do_norm, head_major):
    @pl.when(pl.program_id(1) == 0)
    def _():
        x = x_ref[...]
        if do_norm:
            x = _rms(x, g_ref[...])
        h_ref[...] = x.astype(BF16)

    acc = _dot(h_ref[...], w_ref[...])
    if head_major:
        for c in range(o_ref.shape[0]):
            o_ref[c] = acc[:, c * LANE:(c + 1) * LANE]
    else:
        o_ref[...] = acc


def norm_matmul(x, gain, w, *, tm, tn, do_norm=True, head_major=False):
    m, k = x.shape
    n = w.shape[1]
    assert m % tm == 0 and n % tn == 0 and tn % LANE == 0
    if head_major:
        out_shape = jax.ShapeDtypeStruct((n // LANE, m, LANE), F32)
        out_spec = pl.BlockSpec((tn // LANE, tm, LANE), lambda i, j: (j, i, 0))
    else:
        out_shape = jax.ShapeDtypeStruct((m, n), F32)
        out_spec = pl.BlockSpec((tm, tn), lambda i, j: (i, j))
    return pl.pallas_call(
        functools.partial(_nmm_kernel, do_norm=do_norm, head_major=head_major),
        out_shape=out_shape,
        grid=(m // tm, n // tn),
        in_specs=[
            pl.BlockSpec((tm, k), lambda i, j: (i, 0)),
            pl.BlockSpec((1, k), lambda i, j: (0, 0)),
            pl.BlockSpec((k, tn), lambda i, j: (0, j)),
        ],
        out_specs=out_spec,
        scratch_shapes=[pltpu.VMEM((tm, k), BF16)],
        compiler_params=_params("parallel", "arbitrary"),
        name="norm_matmul",
    )(x, gain.reshape(1, k), w)


def _mrn_kernel(a_ref, w_ref, x_ref, g_ref, o_ref, acc_ref, *, head_major):
    k = pl.program_id(1)

    @pl.when(k == 0)
    def _():
        acc_ref[...] = jnp.zeros_like(acc_ref)

    if head_major:
        a = jnp.concatenate([a_ref[c].astype(BF16) for c in range(a_ref.shape[0])], axis=1)
    else:
        a = a_ref[...].astype(BF16)
    acc_ref[...] += _dot(a, w_ref[...])

    @pl.when(k == pl.num_programs(1) - 1)
    def _():
        o_ref[...] = x_ref[...] + _rms(acc_ref[...], g_ref[...])


def matmul_resnorm(a, w, x, gain, *, tm, tk, head_major):
    m, d = x.shape
    kdim = w.shape[0]
    assert m % tm == 0 and kdim % tk == 0 and tk % LANE == 0
    if head_major:
        a_spec = pl.BlockSpec((tk // LANE, tm, LANE), lambda i, k: (k, i, 0))
    else:
        a_spec = pl.BlockSpec((tm, tk), lambda i, k: (i, k))
    return pl.pallas_call(
        functools.partial(_mrn_kernel, head_major=head_major),
        out_shape=jax.ShapeDtypeStruct((m, d), F32),
        grid=(m // tm, kdim // tk),
        in_specs=[
            a_spec,
            pl.BlockSpec((tk, d), lambda i, k: (k, 0)),
            pl.BlockSpec((tm, d), lambda i, k: (i, 0)),
            pl.BlockSpec((1, d), lambda i, k: (0, 0)),
        ],
        out_specs=pl.BlockSpec((tm, d), lambda i, k: (i, 0)),
        scratch_shapes=[pltpu.VMEM((tm, d), F32)],
        compiler_params=_params("parallel", "arbitrary"),
        name="matmul_resnorm",
    )(a, w, x, gain.reshape(1, d))


def _gelu_tanh(x):
    c = math.sqrt(2.0 / math.pi)
    return x * (0.5 * (1.0 + jnp.tanh(c * (x + 0.044715 * (x * x * x)))))


def _ffn_up_kernel(*refs, seq_len, tm, long_seq):
    if long_seq:
        x_ref, g_ref, wg_ref, wu_ref, cw_ref, cb_ref, act_ref, gt_ref, h_ref, carry_ref = refs
    else:
        x_ref, g_ref, wg_ref, wu_ref, cw_ref, cb_ref, h1_ref, h2_ref, act_ref, gt_ref, h_ref = refs
    i = pl.program_id(0)
    j = pl.program_id(1)

    @pl.when(j == 0)
    def _():
        h_ref[...] = _rms(x_ref[...], g_ref[...]).astype(BF16)

    h = h_ref[...]
    gate = _dot(h, wg_ref[...])
    up = _dot(h, wu_ref[...])
    row = lax.broadcasted_iota(jnp.int32, gate.shape, 0)
    sh1 = pltpu.roll(gate, 1, axis=0)
    sh2 = pltpu.roll(gate, 2, axis=0)
    if long_seq:
        prev = jnp.where((i * tm) % seq_len == 0, 0.0, carry_ref[j])
        carry_ref[j] = gate[tm - SUBLANE:tm]
        gt_ref[0] = gate[tm - SUBLANE:tm]
        sh1 = jnp.where(row == 0, prev[7:8], sh1)
        sh2 = jnp.where(row == 0, prev[6:7], jnp.where(row == 1, prev[7:8], sh2))
    else:
        pos = row % SUBLANE
        gt_ref[...] = gate
        sh1 = jnp.where(pos == 0, 0.0, sh1) + h1_ref[...]
        sh2 = jnp.where(pos < 2, 0.0, sh2) + h2_ref[...]
    y = sh2 * cw_ref[0:1] + sh1 * cw_ref[1:2] + gate * cw_ref[2:3]
    act_ref[...] = (_gelu_tanh(y + cb_ref[...]) * up).astype(BF16)


def ffn_up(x, gain, w_up, conv_w, conv_b, *, seq_len, tm, tn, hist=None):
    m, k = x.shape
    f = conv_w.shape[1]
    nj = f // tn
    assert m % tm == 0 and f % tn == 0
    long_seq = seq_len >= tm
    if long_seq:
        assert seq_len % tm == 0
    else:
        assert seq_len == SUBLANE
    in_specs = [
        pl.BlockSpec((tm, k), lambda i, j: (i, 0)),
        pl.BlockSpec((1, k), lambda i, j: (0, 0)),
        pl.BlockSpec((k, tn), lambda i, j: (0, j)),
        pl.BlockSpec((k, tn), lambda i, j: (0, j + nj)),
        pl.BlockSpec((FFN_CONV, tn), lambda i, j: (0, j)),
        pl.BlockSpec((1, tn), lambda i, j: (0, j)),
    ]
    args = [x, gain.reshape(1, k), w_up, w_up, conv_w, conv_b.reshape(1, f)]
    scratch = [pltpu.VMEM((tm, k), BF16)]
    if long_seq:
        gt_shape = jax.ShapeDtypeStruct((m // tm, SUBLANE, f), F32)
        gt_spec = pl.BlockSpec((1, SUBLANE, tn), lambda i, j: (i, 0, j))
        scratch.append(pltpu.VMEM((nj, SUBLANE, tn), F32))
    else:
        gt_shape = jax.ShapeDtypeStruct((m, f), F32)
        gt_spec = pl.BlockSpec((tm, tn), lambda i, j: (i, j))
        in_specs += [pl.BlockSpec((tm, tn), lambda i, j: (i, j))] * 2
        args += list(hist)
    return pl.pallas_call(
        functools.partial(_ffn_up_kernel, seq_len=seq_len, tm=tm, long_seq=long_seq),
        out_shape=(jax.ShapeDtypeStruct((m, f), BF16), gt_shape),
        grid=(m // tm, nj),
        in_specs=in_specs,
        out_specs=(pl.BlockSpec((tm, tn), lambda i, j: (i, j)), gt_spec),
        scratch_shapes=scratch,
        compiler_params=_params("arbitrary", "arbitrary"),
        name="ffn_up",
    )(*args)


def _gdn_kernel(p_ref, g_ref, cw_ref, gp_ref, gain_ref, tail0_ref, s0_ref, o_ref, s_ref,
                tail_ref, gct_ref, *, chunk, heads_per_iter):
    c = chunk
    nqk = GDN_QK_HEADS
    rep = GDN_V_HEADS // GDN_QK_HEADS

    @pl.when(pl.program_id(1) == 0)
    def _():
        s_ref[...] = s0_ref[...]
        tail_ref[...] = tail0_ref[0]

    gates = g_ref[...]
    a_log = gp_ref[0:1, :]
    dt_bias = gp_ref[1:2, :]
    beta_all = jax.nn.sigmoid(gates[:, :LANE])
    xa = gates[:, LANE:] + dt_bias
    softplus = jnp.maximum(xa, 0.0) + jnp.log1p(jnp.exp(-jnp.abs(xa)))
    g_all = -jnp.exp(a_log) * softplus
    row = lax.broadcasted_iota(jnp.int32, (c, c), 0)
    col = lax.broadcasted_iota(jnp.int32, (c, c), 1)
    causal = row >= col
    strict = row > col
    eye = (row == col).astype(F32)
    gc_all = jnp.dot(causal.astype(F32), g_all, precision=HIGHEST, preferred_element_type=F32)
    gct_ref[...] = gc_all.T
    lane = lax.broadcasted_iota(jnp.int32, (c, LANE), 1)
    gain = gain_ref[...]

    def conv_silu(cb):
        x = p_ref[cb]
        w = cw_ref[cb]
        ext = jnp.concatenate([tail_ref[cb], x], axis=0)
        y = ext[5:5 + c] * w[0:1] + ext[6:6 + c] * w[1:2] + ext[7:7 + c] * w[2:3] + x * w[3:4]
        tail_ref[cb] = x[c - SUBLANE:c]
        return _silu(y)

    def l2norm(x):
        return x * lax.rsqrt(jnp.sum(x * x, axis=-1, keepdims=True) + NORM_EPS)

    def qk_head(j):
        q = l2norm(conv_silu(j)) * (LANE ** -0.5)
        k = l2norm(conv_silu(nqk + j))
        q16 = q.astype(BF16)
        k16 = k.astype(BF16)
        kk = _dot_nt(k16, k16)
        qk = _dot_nt(q16, k16)
        for e in range(rep):
            hv = rep * j + e
            sel = lane == hv
            beta_c = jnp.sum(jnp.where(sel, beta_all, 0.0), axis=1, keepdims=True)
            gc_c = jnp.sum(jnp.where(sel, gc_all, 0.0), axis=1, keepdims=True)
            gc_r = gct_ref[pl.ds(hv, 1), :]
            gl = gc_c[c - 1:c, :]
            diff = gc_c - gc_r
            decay = jnp.where(causal, jnp.exp(jnp.where(causal, diff, 0.0)), 0.0)
            lmat = jnp.where(strict, (beta_c * kk) * decay, 0.0)
            a_intra = qk * decay
            xm = -lmat
            tinv = eye + xm
            for _ in range(int(math.log2(c)) - 1):
                xm = jnp.dot(xm, xm, precision=HIGHEST, preferred_element_type=F32)
                tinv = tinv + jnp.dot(tinv, xm, precision=HIGHEST, preferred_element_type=F32)
            tinv16 = tinv.astype(BF16)
            v = conv_silu(2 * nqk + hv)
            u = _dot(tinv16, (v * beta_c).astype(BF16))
            w = _dot(tinv16, (k * (beta_c * jnp.exp(gc_c))).astype(BF16))
            s_old = s_ref[0, hv]
            s16 = s_old.astype(BF16)
            v_new = u - _dot(w.astype(BF16), s16)
            v_new16 = v_new.astype(BF16)
            o = _dot((q * jnp.exp(gc_c)).astype(BF16), s16) + _dot(a_intra.astype(BF16), v_new16)
            k_dec = k * jnp.exp(gl - gc_c)
            s_ref[0, hv] = s_old * jnp.exp(gl) + _dot_tn(k_dec.astype(BF16), v_new16)
            z = p_ref[2 * nqk + GDN_V_HEADS + hv]
            o_ref[hv] = _rms(o, gain) * _silu(z)

    def body(it, carry):
        for jj in range(heads_per_iter):
            qk_head(it * heads_per_iter + jj)
        return carry

    lax.fori_loop(0, nqk // heads_per_iter, body, 0)


def gdn_core(p, gates, conv_w_hm, gate_params, o_gain, tail0, state0, *, batch, seq_len, heads_per_iter=4):
    nblk, m, _ = p.shape
    c = min(GDN_CHUNK, seq_len)
    assert seq_len % c == 0 and c % SUBLANE == 0 and m == batch * seq_len
    nc = seq_len // c
    ncols = 2 * GDN_QK_HEADS + GDN_V_HEADS
    return pl.pallas_call(
        functools.partial(_gdn_kernel, chunk=c, heads_per_iter=heads_per_iter),
        out_shape=(jax.ShapeDtypeStruct((GDN_V_HEADS, m, LANE), F32),
                   jax.ShapeDtypeStruct((batch, GDN_V_HEADS, LANE, LANE), F32)),
        grid=(batch, nc),
        in_specs=[
            pl.BlockSpec((nblk, c, LANE), lambda b, n: (0, b * nc + n, 0)),
            pl.BlockSpec((c, 2 * LANE), lambda b, n: (b * nc + n, 0)),
            pl.BlockSpec((ncols, SUBLANE, LANE), lambda b, n: (0, 0, 0)),
            pl.BlockSpec((SUBLANE, LANE), lambda b, n: (0, 0)),
            pl.BlockSpec((1, LANE), lambda b, n: (0, 0)),
            pl.BlockSpec((1, ncols, SUBLANE, LANE), lambda b, n: (b, 0, 0, 0)),
            pl.BlockSpec((1, GDN_V_HEADS, LANE, LANE), lambda b, n: (b, 0, 0, 0)),
        ],
        out_specs=(pl.BlockSpec((GDN_V_HEADS, c, LANE), lambda b, n: (0, b * nc + n, 0)),
                   pl.BlockSpec((1, GDN_V_HEADS, LANE, LANE), lambda b, n: (b, 0, 0, 0))),
        scratch_shapes=[pltpu.VMEM((ncols, SUBLANE, LANE), F32), pltpu.VMEM((LANE, c), F32)],
        compiler_params=_params("parallel", "arbitrary"),
        name="gdn_core",
    )(p, gates, conv_w_hm, gate_params, o_gain.reshape(1, LANE), tail0, state0)


def _rope(x, cos_f, sin_f):
    return x * cos_f + pltpu.roll(x, LANE // 2, axis=1) * sin_f


def _swa_prompt_kernel(qkv_ref, cos_ref, sin_ref, o_ref, kr_ref, qs_ref, lse_ref, *, seq_len):
    g = pl.program_id(2)
    cos_f = cos_ref[...]
    sin_f = sin_ref[...]
    qs_ref[...] = _rope(qkv_ref[0, 0, 0], cos_f, sin_f)
    kr_ref[0, 0] = _rope(qkv_ref[1, 0, 0], cos_f, sin_f)
    blk = SWA_SPAN
    ii = lax.broadcasted_iota(jnp.int32, (blk, 2 * blk), 0)
    jj = lax.broadcasted_iota(jnp.int32, (blk, 2 * blk), 1)
    dist = ii + blk - jj
    band = (dist >= 0) & (dist <= SWA_SPAN)
    scale = LANE ** -0.5

    for gi, (window, dil) in enumerate(SWA_GROUPS):
        assert window // dil == SWA_SPAN and seq_len % (blk * dil) == 0
        nblk = seq_len // dil // blk

        @pl.when(g == gi)
        def _(gi=gi, dil=dil, nblk=nblk):
            def unit(u, carry):
                r = u // nblk
                b = u % nblk
                cur = pl.ds(r + b * blk * dil, blk, stride=dil)
                prv = pl.ds(r + jnp.maximum(b - 1, 0) * blk * dil, blk, stride=dil)
                q = qs_ref[cur, :].astype(BF16)
                kcat = jnp.concatenate([kr_ref[0, 0, prv, :], kr_ref[0, 0, cur, :]], axis=0).astype(BF16)
                vcat = jnp.concatenate([qkv_ref[2, 0, 0, prv, :], qkv_ref[2, 0, 0, cur, :]], axis=0).astype(BF16)
                s = _dot_nt(q, kcat) * scale
                first_key = jnp.where(b > 0, 0, blk)
                valid = band & (jj >= first_key)
                s = jnp.where(valid, s, -jnp.inf)
                mx = jnp.max(s, axis=-1, keepdims=True)
                lse = mx + jnp.log(jnp.sum(jnp.exp(s - mx), axis=-1, keepdims=True))
                p = jnp.exp(s - lse)
                o_ref[gi, 0, cur, :] = _dot(p.astype(BF16), vcat)
                lse_ref[gi, cur, :] = jnp.broadcast_to(lse, (blk, LANE))
                return carry

            lax.fori_loop(0, seq_len // blk, unit, 0)

    @pl.when(g == len(SWA_GROUPS) - 1)
    def _():
        lses = [lse_ref[gi] for gi in range(len(SWA_GROUPS))]
        mx = functools.reduce(jnp.maximum, lses)
        es = [jnp.exp(l - mx) for l in lses]
        den = functools.reduce(lambda a, b: a + b, es)
        for gi in range(len(SWA_GROUPS)):
            o_ref[gi, 0] = o_ref[gi, 0] * (es[gi] / den)


def swa_prompt(qkv_hm, cos_f, sin_f, *, batch, seq_len):
    ng = len(SWA_GROUPS)
    m = batch * seq_len
    qkv5 = qkv_hm.reshape(3, ng, SWA_HPG, m, LANE)
    o, kr = pl.pallas_call(
        functools.partial(_swa_prompt_kernel, seq_len=seq_len),
        out_shape=(jax.ShapeDtypeStruct((ng, SWA_HPG, m, LANE), F32),
                   jax.ShapeDtypeStruct((ng, SWA_HPG, m, LANE), F32)),
        grid=(batch, SWA_HPG, ng),
        in_specs=[
            pl.BlockSpec((3, 1, 1, seq_len, LANE), lambda b, h, g: (0, g, h, b, 0)),
            pl.BlockSpec((seq_len, LANE), lambda b, h, g: (0, 0)),
            pl.BlockSpec((seq_len, LANE), lambda b, h, g: (0, 0)),
        ],
        out_specs=(pl.BlockSpec((ng, 1, seq_len, LANE), lambda b, h, g: (0, h, b, 0)),
                   pl.BlockSpec((1, 1, seq_len, LANE), lambda b, h, g: (g, h, b, 0))),
        scratch_shapes=[pltpu.VMEM((seq_len, LANE), F32), pltpu.VMEM((ng, seq_len, LANE), F32)],
        compiler_params=_params("parallel", "parallel", "arbitrary"),
        name="swa_prompt",
    )(qkv5, cos_f, sin_f)
    return o.reshape(ng * SWA_HPG, m, LANE), kr


def _mem_prompt_kernel(q_ref, k_ref, v_ref, o_ref):
    s = _dot_nt(q_ref[0].astype(BF16), k_ref[0].astype(BF16)) * (LANE ** -0.5)
    mx = jnp.max(s, axis=-1, keepdims=True)
    e = jnp.exp(s - mx)
    p = e / jnp.sum(e, axis=-1, keepdims=True)
    o_ref[0] = _dot(p.astype(BF16), v_ref[0].astype(BF16))


def mem_prompt(q_hm, kv_hm, *, batch, seq_len, mem_len, tq):
    m = batch * seq_len
    nq = seq_len // tq
    return pl.pallas_call(
        _mem_prompt_kernel,
        out_shape=jax.ShapeDtypeStruct((MEM_HEADS, m, LANE), F32),
        grid=(batch, MEM_HEADS, nq),
        in_specs=[
            pl.BlockSpec((1, tq, LANE), lambda b, h, i: (h, b * nq + i, 0)),
            pl.BlockSpec((1, mem_len, LANE), lambda b, h, i: (h, b, 0)),
            pl.BlockSpec((1, mem_len, LANE), lambda b, h, i: (MEM_HEADS + h, b, 0)),
        ],
        out_specs=pl.BlockSpec((1, tq, LANE), lambda b, h, i: (h, b * nq + i, 0)),
        compiler_params=_params("parallel", "parallel", "parallel"),
        name="mem_prompt",
    )(q_hm, kv_hm, kv_hm)


def _tile_scores(kt, qt, scale):
    return jnp.sum(kt * qt[None], axis=-1, keepdims=True) * scale


def _mem_decode_kernel(q_ref, c_ref, o_ref):
    kt = c_ref[0, :, 0]
    vt = c_ref[0, :, 1]
    for s in range(q_ref.shape[1]):
        sc = _tile_scores(kt, q_ref[0, s], LANE ** -0.5)
        mx = jnp.max(sc, axis=0, keepdims=True)
        e = jnp.exp(sc - mx)
        p = e / jnp.sum(e, axis=0, keepdims=True)
        o_ref[0, s] = jnp.sum(p * vt, axis=0)


def mem_decode(q4, cache):
    bd, s, h, _ = q4.shape
    mem_len = cache.shape[1]
    return pl.pallas_call(
        _mem_decode_kernel,
        out_shape=jax.ShapeDtypeStruct((bd, s, h, LANE), F32),
        grid=(bd,),
        in_specs=[
            pl.BlockSpec((1, s, h, LANE), lambda b: (b, 0, 0, 0)),
            pl.BlockSpec((1, mem_len, 2, h, LANE), lambda b: (b, 0, 0, 0, 0)),
        ],
        out_specs=pl.BlockSpec((1, s, h, LANE), lambda b: (b, 0, 0, 0)),
        compiler_params=_params("parallel"),
        name="mem_decode",
    )(q4, cache)


def _swa_decode_kernel(qkv_ref, cos_ref, sin_ref, c0_ref, c1_ref, c2_ref,
                       o_ref, n0_ref, n1_ref, n2_ref, *, steps):
    ng = len(SWA_GROUPS)
    caches = (c0_ref, c1_ref, c2_ref)
    news = (n0_ref, n1_ref, n2_ref)
    scale = LANE ** -0.5
    q_t, k_t, v_t = {}, {}, {}
    for s in range(steps):
        cos_f = cos_ref[s:s + 1, :]
        sin_f = sin_ref[s:s + 1, :]
        for g in range(ng):
            q_t[s, g] = _rope(qkv_ref[0, s, g], cos_f, sin_f)
            k_t[s, g] = _rope(qkv_ref[0, s, ng + g], cos_f, sin_f)
            v_t[s, g] = qkv_ref[0, s, 2 * ng + g]
            news[g][0, s, 0] = k_t[s, g]
            news[g][0, s, 1] = v_t[s, g]

    outs, lses = {}, {}
    for g, (window, dil) in enumerate(SWA_GROUPS):
        nres = caches[g].shape[2]
        for r in range(nres):
            kt = caches[g][0, :, r, 0]
            vt = caches[g][0, :, r, 1]
            jb = lax.broadcasted_iota(jnp.int32, (kt.shape[0], kt.shape[1], 1), 0)
            for s in range(r, steps, dil):
                first = s // dil
                sc = _tile_scores(kt, q_t[s, g], scale)
                sc = jnp.where(jb >= first, sc, -jnp.inf)
                new_rows = [m * dil + r for m in range(first + 1)]
                sn = [jnp.sum(k_t[t, g] * q_t[s, g], axis=-1, keepdims=True) * scale for t in new_rows]
                mx = jnp.max(sc, axis=0)
                for x in sn:
                    mx = jnp.maximum(mx, x)
                e = jnp.exp(sc - mx[None])
                en = [jnp.exp(x - mx) for x in sn]
                den = jnp.sum(e, axis=0)
                acc = jnp.sum(e * vt, axis=0)
                for x, t in zip(en, new_rows):
                    den = den + x
                    acc = acc + x * v_t[t, g]
                outs[s, g] = acc / den
                lses[s, g] = mx + jnp.log(den)

    for s in range(steps):
        mx = functools.reduce(jnp.maximum, [lses[s, g] for g in range(ng)])
        es = [jnp.exp(lses[s, g] - mx) for g in range(ng)]
        den = functools.reduce(lambda a, b: a + b, es)
        for g in range(ng):
            o_ref[0, s, g] = outs[s, g] * (es[g] / den)


def swa_decode(qkv5, cos_f, sin_f, caches):
    bd, steps = qkv5.shape[:2]
    ng = len(SWA_GROUPS)
    views, specs = [], []
    for (window, dil), c in zip(SWA_GROUPS, caches):
        lb = c.shape[1]
        assert lb == window and lb // dil == SWA_SPAN, "cache must hold exactly one window"
        nres = min(dil, steps)
        assert steps <= dil or steps % dil == 0
        views.append(c.reshape(bd, lb // dil, dil, 2, SWA_HPG, LANE))
        specs.append(pl.BlockSpec((1, lb // dil, nres, 2, SWA_HPG, LANE), lambda b: (b, 0, 0, 0, 0, 0)))
    new_shape = jax.ShapeDtypeStruct((bd, steps, 2, SWA_HPG, LANE), F32)
    new_spec = pl.BlockSpec((1, steps, 2, SWA_HPG, LANE), lambda b: (b, 0, 0, 0, 0))
    return pl.pallas_call(
        functools.partial(_swa_decode_kernel, steps=steps),
        out_shape=(jax.ShapeDtypeStruct((bd, steps, ng, SWA_HPG, LANE), F32), new_shape, new_shape, new_shape),
        grid=(bd,),
        in_specs=[
            pl.BlockSpec((1, steps, 3 * ng, SWA_HPG, LANE), lambda b: (b, 0, 0, 0, 0)),
            pl.BlockSpec((steps, LANE), lambda b: (0, 0)),
            pl.BlockSpec((steps, LANE), lambda b: (0, 0)),
        ] + specs,
        out_specs=(pl.BlockSpec((1, steps, ng, SWA_HPG, LANE), lambda b: (b, 0, 0, 0, 0)),
                   new_spec, new_spec, new_spec),
        compiler_params=_params("parallel"),
        name="swa_decode",
    )(qkv5, cos_f, sin_f, *views)


def _cache_roll_kernel(*refs, steps):
    n = (len(refs) - 1) // 3
    caches, news, outs, sem = refs[:n], refs[n:2 * n], refs[2 * n:3 * n], refs[3 * n]

    def copies(i):
        keep = caches[i].shape[1] - steps
        return (pltpu.make_async_copy(caches[i].at[:, pl.ds(steps, keep)], outs[i].at[:, pl.ds(0, keep)], sem.at[i, 0]),
                pltpu.make_async_copy(news[i], outs[i].at[:, pl.ds(keep, steps)], sem.at[i, 1]))

    for i in range(n):
        for cp in copies(i):
            cp.start()
    for i in range(n):
        for cp in copies(i):
            cp.wait()


def cache_roll(caches, news):
    n = len(caches)
    steps = news[0].shape[1]
    any_spec = pl.BlockSpec(memory_space=pl.ANY)
    return pl.pallas_call(
        functools.partial(_cache_roll_kernel, steps=steps),
        out_shape=tuple(jax.ShapeDtypeStruct(c.shape, c.dtype) for c in caches),
        in_specs=[any_spec] * (2 * n),
        out_specs=tuple([any_spec] * n),
        scratch_shapes=[pltpu.SemaphoreType.DMA((n, 2))],
        name="cache_roll",
    )(*caches, *news)


def _rope_tables(pos):
    half = LANE // 2
    inv = ROPE_THETA ** (-jnp.arange(half, dtype=F32) / half)
    ang = pos.astype(F32)[:, None] * inv[None, :]
    cos, sin = jnp.cos(ang), jnp.sin(ang)
    return jnp.concatenate([cos, cos], axis=1), jnp.concatenate([-sin, sin], axis=1)


def _hm_to_tokens(x_hm, batch, seq_len):
    h = x_hm.shape[0]
    return x_hm.reshape(h, batch, seq_len, LANE).transpose(1, 2, 0, 3)


def _tile_m(m):
    return 1024 if m % 1024 == 0 else m


def kernel(x_prompt, x_sample, state_gdn, state_gdn_conv, cache_swa_kv0, cache_swa_kv1, cache_swa_kv2,
           cache_mem_kv, state_ffn_conv, mem_prompt_in, norm_gains, gdn_w_in, gdn_conv_w, gdn_a_log,
           gdn_dt_bias, gdn_norm, gdn_w_out, swa_w_qkv, swa_w_out, mem_w_q, mem_w_kv, mem_w_o,
           ffn_w_up, ffn_conv_w, ffn_conv_b, ffn_w_down):
    depth = norm_gains.shape[0]
    d_model = x_prompt.shape[-1]
    qk_dim = GDN_QK_HEADS * LANE
    v_dim = GDN_V_HEADS * LANE
    conv_dim = 2 * qk_dim + v_dim
    main_dim = conv_dim + v_dim
    ncols = conv_dim // LANE

    def gdn_weights(ia):
        w_in = gdn_w_in[ia]
        w_gate = jnp.zeros((d_model, 2 * LANE), F32)
        w_gate = w_gate.at[:, :GDN_V_HEADS].set(w_in[:, main_dim:main_dim + GDN_V_HEADS])
        w_gate = w_gate.at[:, LANE:LANE + GDN_V_HEADS].set(w_in[:, main_dim + GDN_V_HEADS:])
        cw = gdn_conv_w[ia].reshape(GDN_CONV, ncols, LANE).transpose(1, 0, 2)
        cw = jnp.pad(cw, ((0, 0), (0, SUBLANE - GDN_CONV), (0, 0)))
        gp = jnp.zeros((SUBLANE, LANE), F32)
        gp = gp.at[0, :GDN_V_HEADS].set(gdn_a_log[ia]).at[1, :GDN_V_HEADS].set(gdn_dt_bias[ia])
        return (w_in[:, :main_dim].astype(BF16), w_gate.astype(BF16), cw, gp, gdn_norm[ia],
                gdn_w_out[ia].astype(BF16))

    def run_trunk(x3, pos, prompt, mem_src):
        batch, seq_len, _ = x3.shape
        m = batch * seq_len
        x = x3.reshape(m, d_model)
        tm1 = _tile_m(m)
        tm2 = 512 if m % 512 == 0 else m
        cos_f, sin_f = _rope_tables(pos)
        out = {"gdn_s": [], "gdn_c": [], "swa": [[] for _ in SWA_GROUPS], "ffn": [], "mem_kv": []}
        ia = ib = 0
        for layer in range(depth):
            g = norm_gains[layer]
            if layer % 2 == 0:
                w_main, w_gate, cw, gp, o_gain, w_out = gdn_weights(ia)
                p = norm_matmul(x, g[0], w_main, tm=tm1, tn=512, head_major=True)
                gates = norm_matmul(x, g[0], w_gate, tm=tm1, tn=2 * LANE)
                if prompt:
                    tail0 = jnp.zeros((batch, ncols, SUBLANE, LANE), F32)
                    s0 = jnp.zeros((batch, GDN_V_HEADS, LANE, LANE), F32)
                else:
                    cb = state_gdn_conv[ia].reshape(batch, GDN_CONV - 1, ncols, LANE).transpose(0, 2, 1, 3)
                    tail0 = jnp.pad(cb, ((0, 0), (0, 0), (SUBLANE - (GDN_CONV - 1), 0), (0, 0)))
                    s0 = state_gdn[ia]
                o_hm, s1 = gdn_core(p, gates, cw, gp, o_gain, tail0, s0, batch=batch, seq_len=seq_len)
                out["gdn_s"].append(s1)
                c1 = p[:ncols].reshape(ncols, batch, seq_len, LANE)[:, :, seq_len - (GDN_CONV - 1):]
                out["gdn_c"].append(c1.transpose(1, 2, 0, 3).reshape(batch, GDN_CONV - 1, conv_dim))
                x = matmul_resnorm(o_hm, w_out, x, g[1], tm=tm2, tk=1024, head_major=True)
                ia += 1
            else:
                w_qkv = swa_w_qkv[ib].astype(BF16)
                w_out = swa_w_out[ib].astype(BF16)
                ng = len(SWA_GROUPS)
                if prompt:
                    qkv_hm = norm_matmul(x, g[0], w_qkv, tm=tm1, tn=768, head_major=True)
                    o_hm, kr = swa_prompt(qkv_hm, cos_f, sin_f, batch=batch, seq_len=seq_len)
                    v_hm = qkv_hm.reshape(3, ng, SWA_HPG, m, LANE)[2]
                    for gi, (window, _) in enumerate(SWA_GROUPS):
                        keep = min(window, seq_len)
                        kv = jnp.stack([_hm_to_tokens(kr[gi], batch, seq_len),
                                        _hm_to_tokens(v_hm[gi], batch, seq_len)], axis=2)
                        out["swa"][gi].append(kv[:, seq_len - keep:])
                    x = matmul_resnorm(o_hm, w_out, x, g[1], tm=tm2, tk=768, head_major=True)
                else:
                    qkv = norm_matmul(x, g[0], w_qkv, tm=tm1, tn=768)
                    qkv5 = qkv.reshape(batch, seq_len, 3 * ng, SWA_HPG, LANE)
                    caches = [c[ib] for c in (cache_swa_kv0, cache_swa_kv1, cache_swa_kv2)]
                    o5, n0, n1, n2 = swa_decode(qkv5, cos_f, sin_f, caches)
                    rolled = cache_roll(caches, [n0, n1, n2])
                    for gi in range(ng):
                        out["swa"][gi].append(rolled[gi])
                    x = matmul_resnorm(o5.reshape(m, ng * SWA_HPG * LANE), w_out, x, g[1],
                                       tm=tm2, tk=768, head_major=False)
                ib += 1
            w_q = mem_w_q[layer].astype(BF16)
            w_o = mem_w_o[layer].astype(BF16)
            if prompt:
                mem_len = mem_src.shape[1]
                kv_hm = norm_matmul(mem_src.reshape(batch * mem_len, d_model), g[2], mem_w_kv[layer].astype(BF16),
                                    tm=_tile_m(batch * mem_len), tn=512, do_norm=False, head_major=True)
                out["mem_kv"].append(kv_hm.reshape(2, MEM_HEADS, batch, mem_len, LANE).transpose(2, 3, 0, 1, 4))
                q_hm = norm_matmul(x, g[2], w_q, tm=tm1, tn=512, head_major=True)
                o_hm = mem_prompt(q_hm, kv_hm, batch=batch, seq_len=seq_len, mem_len=mem_len, tq=1024)
                x = matmul_resnorm(o_hm, w_o, x, g[3], tm=tm2, tk=512, head_major=True)
            else:
                q = norm_matmul(x, g[2], w_q, tm=tm1, tn=512)
                o4 = mem_decode(q.reshape(batch, seq_len, MEM_HEADS, LANE), mem_src[layer])
                x = matmul_resnorm(o4.reshape(m, MEM_HEADS * LANE), w_o, x, g[3], tm=tm2, tk=512, head_major=False)
            w_up = ffn_w_up[layer].astype(BF16)
            w_down = ffn_w_down[layer].astype(BF16)
            if prompt:
                act, gt = ffn_up(x, g[4], w_up, ffn_conv_w[layer], ffn_conv_b[layer], seq_len=seq_len, tm=tm1, tn=512)
                per_seq = seq_len // tm1
                out["ffn"].append(gt[per_seq - 1::per_seq, SUBLANE - (FFN_CONV - 1):])
            else:
                buf = state_ffn_conv[layer]
                zero = jnp.zeros((batch, seq_len - 1, buf.shape[-1]), F32)
                hist1 = jnp.concatenate([buf[:, 1:2], zero], axis=1).reshape(m, -1)
                hist2 = jnp.concatenate([buf, zero[:, 1:]], axis=1).reshape(m, -1)
                act, gt = ffn_up(x, g[4], w_up, ffn_conv_w[layer], ffn_conv_b[layer], seq_len=seq_len, tm=tm1,
                                 tn=512, hist=(hist1, hist2))
                out["ffn"].append(gt.reshape(batch, seq_len, -1)[:, seq_len - (FFN_CONV - 1):])
            x = matmul_resnorm(act, w_down, x, g[5], tm=tm2, tk=1408, head_major=False)
        return x.reshape(batch, seq_len, d_model), out

    pos_p = jnp.arange(x_prompt.shape[1])
    y_p, op = run_trunk(x_prompt, pos_p, True, mem_prompt_in)
    pos_s = PAST_LEN + jnp.arange(x_sample.shape[1])
    y_s, os_ = run_trunk(x_sample, pos_s, False, cache_mem_kv)

    return (y_p, y_s, jnp.stack(op["gdn_s"]), jnp.stack(op["gdn_c"]),
            jnp.stack(op["swa"][0]), jnp.stack(op["swa"][1]), jnp.stack(op["swa"][2]),
            jnp.stack(op["mem_kv"]), jnp.stack(op["ffn"]),
            jnp.stack(os_["gdn_s"]), jnp.stack(os_["gdn_c"]),
            jnp.stack(os_["swa"][0]), jnp.stack(os_["swa"][1]), jnp.stack(os_["swa"][2]),
            jnp.stack(os_["ffn"]))
```

```python
import functools
import math

import jax
import jax.numpy as jnp
from jax import lax
from jax.experimental import pallas as pl
from jax.experimental.pallas import tpu as pltpu

F32 = jnp.float32
BF16 = jnp.bfloat16
HIGHEST = lax.Precision.HIGHEST

LANE = 128
SUBLANE = 8
NORM_EPS = 1e-6
ROPE_THETA = 10000.0
PAST_LEN = 2048
GDN_CHUNK = 64
GDN_QK_HEADS = 16
GDN_V_HEADS = 32
GDN_CONV = 4
SWA_GROUPS = ((128, 1), (512, 4), (2048, 16))
SWA_HPG = 6
SWA_SPAN = 128
MEM_HEADS = 4
FFN_CONV = 3
VMEM_LIMIT_BYTES = 52 * 1024 * 1024


def _params(*semantics):
    return pltpu.CompilerParams(dimension_semantics=semantics, vmem_limit_bytes=VMEM_LIMIT_BYTES)


def _dot(a, b):
    return jnp.dot(a, b, preferred_element_type=F32)


def _dot_nt(a, b):
    return lax.dot_general(a, b, (((1,), (1,)), ((), ())), preferred_element_type=F32)


def _dot_tn(a, b):
    return lax.dot_general(a, b, (((0,), (0,)), ((), ())), preferred_element_type=F32)


def _rms(x, gain):
    return x * lax.rsqrt(jnp.mean(x * x, axis=-1, keepdims=True) + NORM_EPS) * gain


def _silu(x):
    return x * jax.nn.sigmoid(x)


def _nmm_kernel(x_ref, g_ref, w_ref, o_ref, h_ref, *, do_norm, head_major):
    @pl.when(pl.program_id(1) == 0)
    def _():
        x = x_ref[...]
        if do_norm:
            x = _rms(x, g_ref[...])
        h_ref[...] = x.astype(BF16)

    acc = _dot(h_ref[...], w_ref[...])
    if head_major:
        for c in range(o_ref.shape[0]):
            o_ref[c] = acc[:, c * LANE:(c + 1) * LANE]
    else:
        o_ref[...] = acc


def norm_matmul(x, gain, w, *, tm, tn, do_norm=True, head_major=False):
    m, k = x.shape
    n = w.shape[1]
    assert m % tm == 0 and n % tn == 0 and tn % LANE == 0
    if head_major:
        out_shape = jax.ShapeDtypeStruct((n // LANE, m, LANE), F32)
        out_spec = pl.BlockSpec((tn // LANE, tm, LANE), lambda i, j: (j, i, 0))
    else:
        out_shape = jax.ShapeDtypeStruct((m, n), F32)
        out_spec = pl.BlockSpec((tm, tn), lambda i, j: (i, j))
    return pl.pallas_call(
        functools.partial(_nmm_kernel, do_norm=do_norm, head_major=head_major),
        out_shape=out_shape,
        grid=(m // tm, n // tn),
        in_specs=[
            pl.BlockSpec((tm, k), lambda i, j: (i, 0)),
            pl.BlockSpec((1, k), lambda i, j: (0, 0)),
            pl.BlockSpec((k, tn), lambda i, j: (0, j)),
        ],
        out_specs=out_spec,
        scratch_shapes=[pltpu.VMEM((tm, k), BF16)],
        compiler_params=_params("parallel", "arbitrary"),
        name="norm_matmul",
    )(x, gain.reshape(1, k), w)


def _mrn_kernel(a_ref, w_ref, x_ref, g_ref, o_ref, acc_ref, *, head_major):
    k = pl.program_id(1)

    @pl.when(k == 0)
    def _():
        acc_ref[...] = jnp.zeros_like(acc_ref)

    if head_major:
        a = jnp.concatenate([a_ref[c].astype(BF16) for c in range(a_ref.shape[0])], axis=1)
    else:
        a = a_ref[...].astype(BF16)
    acc_ref[...] += _dot(a, w_ref[...])

    @pl.when(k == pl.num_programs(1) - 1)
    def _():
        o_ref[...] = x_ref[...] + _rms(acc_ref[...], g_ref[...])


def matmul_resnorm(a, w, x, gain, *, tm, tk, head_major):
    m, d = x.shape
    kdim = w.shape[0]
    assert m % tm == 0 and kdim % tk == 0 and tk % LANE == 0
    if head_major:
        a_spec = pl.BlockSpec((tk // LANE, tm, LANE), lambda i, k: (k, i, 0))
    else:
        a_spec = pl.BlockSpec((tm, tk), lambda i, k: (i, k))
    return pl.pallas_call(
        functools.partial(_mrn_kernel, head_major=head_major),
        out_shape=jax.ShapeDtypeStruct((m, d), F32),
        grid=(m // tm, kdim // tk),
        in_specs=[
            a_spec,
            pl.BlockSpec((tk, d), lambda i, k: (k, 0)),
            pl.BlockSpec((tm, d), lambda i, k: (i, 0)),
            pl.BlockSpec((1, d), lambda i, k: (0, 0)),
        ],
        out_specs=pl.BlockSpec((tm, d), lambda i, k: (i, 0)),
        scratch_shapes=[pltpu.VMEM((tm, d), F32)],
        compiler_params=_params("parallel", "arbitrary"),
        name="matmul_resnorm",
    )(a, w, x, gain.reshape(1, d))


def _gelu_tanh(x):
    c = math.sqrt(2.0 / math.pi)
    return x * (0.5 * (1.0 + jnp.tanh(c * (x + 0.044715 * (x * x * x)))))


def _ffn_up_kernel(*refs, seq_len, tm, long_seq):
    if long_seq:
        x_ref, g_ref, wg_ref, wu_ref, cw_ref, cb_ref, act_ref, gt_ref, h_ref, carry_ref = refs
    else:
        x_ref, g_ref, wg_ref, wu_ref, cw_ref, cb_ref, h1_ref, h2_ref, act_ref, gt_ref, h_ref = refs
    i = pl.program_id(0)
    j = pl.program_id(1)

    @pl.when(j == 0)
    def _():
        h_ref[...] = _rms(x_ref[...], g_ref[...]).astype(BF16)

    h = h_ref[...]
    gate = _dot(h, wg_ref[...])
    up = _dot(h, wu_ref[...])
    row = lax.broadcasted_iota(jnp.int32, gate.shape, 0)
    sh1 = pltpu.roll(gate, 1, axis=0)
    sh2 = pltpu.roll(gate, 2, axis=0)
    if long_seq:
        prev = jnp.where((i * tm) % seq_len == 0, 0.0, carry_ref[j])
        carry_ref[j] = gate[tm - SUBLANE:tm]
        gt_ref[0] = gate[tm - SUBLANE:tm]
        sh1 = jnp.where(row == 0, prev[7:8], sh1)
        sh2 = jnp.where(row == 0, prev[6:7], jnp.where(row == 1, prev[7:8], sh2))
    else:
        pos = row % SUBLANE
        gt_ref[...] = gate
        sh1 = jnp.where(pos == 0, 0.0, sh1) + h1_ref[...]
        sh2 = jnp.where(pos < 2, 0.0, sh2) + h2_ref[...]
    y = sh2 * cw_ref[0:1] + sh1 * cw_ref[1:2] + gate * cw_ref[2:3]
    act_ref[...] = (_gelu_tanh(y + cb_ref[...]) * up).astype(BF16)


def ffn_up(x, gain, w_up, conv_w, conv_b, *, seq_len, tm, tn, hist=None):
    m, k = x.shape
    f = conv_w.shape[1]
    nj = f // tn
    assert m % tm == 0 and f % tn == 0
    long_seq = seq_len >= tm
    if long_seq:
        assert seq_len % tm == 0
    else:
        assert seq_len == SUBLANE
    in_specs = [
        pl.BlockSpec((tm, k), lambda i, j: (i, 0)),
        pl.BlockSpec((1, k), lambda i, j: (0, 0)),
        pl.BlockSpec((k, tn), lambda i, j: (0, j)),
        pl.BlockSpec((k, tn), lambda i, j: (0, j + nj)),
        pl.BlockSpec((FFN_CONV, tn), lambda i, j: (0, j)),
        pl.BlockSpec((1, tn), lambda i, j: (0, j)),
    ]
    args = [x, gain.reshape(1, k), w_up, w_up, conv_w, conv_b.reshape(1, f)]
    scratch = [pltpu.VMEM((tm, k), BF16)]
    if long_seq:
        gt_shape = jax.ShapeDtypeStruct((m // tm, SUBLANE, f), F32)
        gt_spec = pl.BlockSpec((1, SUBLANE, tn), lambda i, j: (i, 0, j))
        scratch.append(pltpu.VMEM((nj, SUBLANE, tn), F32))
    else:
        gt_shape = jax.ShapeDtypeStruct((m, f), F32)
        gt_spec = pl.BlockSpec((tm, tn), lambda i, j: (i, j))
        in_specs += [pl.BlockSpec((tm, tn), lambda i, j: (i, j))] * 2
        args += list(hist)
    return pl.pallas_call(
        functools.partial(_ffn_up_kernel, seq_len=seq_len, tm=tm, long_seq=long_seq),
        out_shape=(jax.ShapeDtypeStruct((m, f), BF16), gt_shape),
        grid=(m // tm, nj),
        in_specs=in_specs,
        out_specs=(pl.BlockSpec((tm, tn), lambda i, j: (i, j)), gt_spec),
        scratch_shapes=scratch,
        compiler_params=_params("arbitrary", "arbitrary"),
        name="ffn_up",
    )(*args)


def _gdn_kernel(p_ref, g_ref, cw_ref, gp_ref, gain_ref, tail0_ref, s0_ref, o_ref, s_ref,
                tail_ref, gct_ref, *, chunk, heads_per_group):
    c = chunk
    nqk = GDN_QK_HEADS
    rep = GDN_V_HEADS // GDN_QK_HEADS

    @pl.when(pl.program_id(1) == 0)
    def _():
        s_ref[...] = s0_ref[...]
        tail_ref[...] = tail0_ref[0]

    gates = g_ref[...]
    a_log = gp_ref[0:1, :]
    dt_bias = gp_ref[1:2, :]
    beta_all = jax.nn.sigmoid(gates[:, :LANE])
    xa = gates[:, LANE:] + dt_bias
    softplus = jnp.maximum(xa, 0.0) + jnp.log1p(jnp.exp(-jnp.abs(xa)))
    g_all = -jnp.exp(a_log) * softplus
    row = lax.broadcasted_iota(jnp.int32, (c, c), 0)
    col = lax.broadcasted_iota(jnp.int32, (c, c), 1)
    causal = row >= col
    strict = row > col
    eye = (row == col).astype(F32)
    gc_all = jnp.dot(causal.astype(F32), g_all, precision=HIGHEST, preferred_element_type=F32)
    gct_ref[...] = gc_all.T
    gain = gain_ref[...]

    def conv_silu(cb):
        x = p_ref[cb]
        w = cw_ref[cb]
        ext = jnp.concatenate([tail_ref[cb], x], axis=0)
        y = ext[5:5 + c] * w[0:1] + ext[6:6 + c] * w[1:2] + ext[7:7 + c] * w[2:3] + x * w[3:4]
        tail_ref[cb] = x[c - SUBLANE:c]
        return _silu(y)

    def l2norm(x):
        return x * lax.rsqrt(jnp.sum(x * x, axis=-1, keepdims=True) + NORM_EPS)

    def head_group(qk_heads):
        v_heads = [rep * j + e for j in qk_heads for e in range(rep)]
        q, k, kk, qk = {}, {}, {}, {}
        for j in qk_heads:
            q[j] = l2norm(conv_silu(j)) * (LANE ** -0.5)
            k[j] = l2norm(conv_silu(nqk + j))
        for j in qk_heads:
            k16 = k[j].astype(BF16)
            kk[j] = _dot_nt(k16, k16)
            qk[j] = _dot_nt(q[j].astype(BF16), k16)
        beta_c, gc_c, a_intra, xm, tinv = {}, {}, {}, {}, {}
        for hv in v_heads:
            j = hv // rep
            beta_c[hv] = beta_all[:, hv:hv + 1]
            gc_c[hv] = gc_all[:, hv:hv + 1]
            diff = gc_c[hv] - gct_ref[hv:hv + 1, :]
            decay = jnp.where(causal, jnp.exp(jnp.where(causal, diff, 0.0)), 0.0)
            a_intra[hv] = (qk[j] * decay).astype(BF16)
            xm[hv] = -jnp.where(strict, (beta_c[hv] * kk[j]) * decay, 0.0)
            tinv[hv] = eye + xm[hv]
        for _ in range(int(math.log2(c)) - 1):
            for hv in v_heads:
                xm16 = xm[hv].astype(BF16)
                xm[hv] = _dot(xm16, xm16)
            for hv in v_heads:
                tinv[hv] = tinv[hv] + _dot(tinv[hv].astype(BF16), xm[hv].astype(BF16))
        u, w = {}, {}
        for hv in v_heads:
            j = hv // rep
            tinv16 = tinv[hv].astype(BF16)
            v = conv_silu(2 * nqk + hv)
            u[hv] = _dot(tinv16, (v * beta_c[hv]).astype(BF16))
            w[hv] = _dot(tinv16, (k[j] * (beta_c[hv] * jnp.exp(gc_c[hv]))).astype(BF16))
        v_new16, o_inter = {}, {}
        for hv in v_heads:
            j = hv // rep
            s16 = s_ref[0, hv].astype(BF16)
            v_new16[hv] = (u[hv] - _dot(w[hv].astype(BF16), s16)).astype(BF16)
            o_inter[hv] = _dot((q[j] * jnp.exp(gc_c[hv])).astype(BF16), s16)
        for hv in v_heads:
            j = hv // rep
            gl = gc_c[hv][c - 1:c, :]
            o = o_inter[hv] + _dot(a_intra[hv], v_new16[hv])
            k_dec = k[j] * jnp.exp(gl - gc_c[hv])
            s_ref[0, hv] = s_ref[0, hv] * jnp.exp(gl) + _dot_tn(k_dec.astype(BF16), v_new16[hv])
            z = p_ref[2 * nqk + GDN_V_HEADS + hv]
            o_ref[hv] = _rms(o, gain) * _silu(z)

    for j0 in range(0, nqk, heads_per_group):
        head_group(range(j0, j0 + heads_per_group))


def gdn_core(p, gates, conv_w_hm, gate_params, o_gain, tail0, state0, *, batch, seq_len, heads_per_group=8):
    nblk, m, _ = p.shape
    c = min(GDN_CHUNK, seq_len)
    assert seq_len % c == 0 and c % SUBLANE == 0 and m == batch * seq_len
    nc = seq_len // c
    ncols = 2 * GDN_QK_HEADS + GDN_V_HEADS
    return pl.pallas_call(
        functools.partial(_gdn_kernel, chunk=c, heads_per_group=heads_per_group),
        out_shape=(jax.ShapeDtypeStruct((GDN_V_HEADS, m, LANE), F32),
                   jax.ShapeDtypeStruct((batch, GDN_V_HEADS, LANE, LANE), F32)),
        grid=(batch, nc),
        in_specs=[
            pl.BlockSpec((nblk, c, LANE), lambda b, n: (0, b * nc + n, 0)),
            pl.BlockSpec((c, 2 * LANE), lambda b, n: (b * nc + n, 0)),
            pl.BlockSpec((ncols, SUBLANE, LANE), lambda b, n: (0, 0, 0)),
            pl.BlockSpec((SUBLANE, LANE), lambda b, n: (0, 0)),
            pl.BlockSpec((1, LANE), lambda b, n: (0, 0)),
            pl.BlockSpec((1, ncols, SUBLANE, LANE), lambda b, n: (b, 0, 0, 0)),
            pl.BlockSpec((1, GDN_V_HEADS, LANE, LANE), lambda b, n: (b, 0, 0, 0)),
        ],
        out_specs=(pl.BlockSpec((GDN_V_HEADS, c, LANE), lambda b, n: (0, b * nc + n, 0)),
                   pl.BlockSpec((1, GDN_V_HEADS, LANE, LANE), lambda b, n: (b, 0, 0, 0))),
        scratch_shapes=[pltpu.VMEM((ncols, SUBLANE, LANE), F32), pltpu.VMEM((LANE, c), F32)],
        compiler_params=_params("parallel", "arbitrary"),
        name="gdn_core",
    )(p, gates, conv_w_hm, gate_params, o_gain.reshape(1, LANE), tail0, state0)


def _rope(x, cos_f, sin_f):
    return x * cos_f + pltpu.roll(x, LANE // 2, axis=1) * sin_f


def _swa_prompt_kernel(qkv_ref, cos_ref, sin_ref, o_ref, kr_ref, qs_ref, lse_ref, *, seq_len):
    g = pl.program_id(2)
    cos_f = cos_ref[...]
    sin_f = sin_ref[...]
    qs_ref[...] = _rope(qkv_ref[0, 0, 0], cos_f, sin_f)
    kr_ref[0, 0] = _rope(qkv_ref[1, 0, 0], cos_f, sin_f)
    blk = SWA_SPAN
    ii = lax.broadcasted_iota(jnp.int32, (blk, 2 * blk), 0)
    jj = lax.broadcasted_iota(jnp.int32, (blk, 2 * blk), 1)
    dist = ii + blk - jj
    band = (dist >= 0) & (dist <= SWA_SPAN)
    scale = LANE ** -0.5

    for gi, (window, dil) in enumerate(SWA_GROUPS):
        assert window // dil == SWA_SPAN and seq_len % (blk * dil) == 0
        nblk = seq_len // dil // blk

        @pl.when(g == gi)
        def _(gi=gi, dil=dil, nblk=nblk):
            def unit(u, carry):
                r = u // nblk
                b = u % nblk
                cur = pl.ds(r + b * blk * dil, blk, stride=dil)
                prv = pl.ds(r + jnp.maximum(b - 1, 0) * blk * dil, blk, stride=dil)
                q = qs_ref[cur, :].astype(BF16)
                kcat = jnp.concatenate([kr_ref[0, 0, prv, :], kr_ref[0, 0, cur, :]], axis=0).astype(BF16)
                vcat = jnp.concatenate([qkv_ref[2, 0, 0, prv, :], qkv_ref[2, 0, 0, cur, :]], axis=0).astype(BF16)
                s = _dot_nt(q, kcat) * scale
                first_key = jnp.where(b > 0, 0, blk)
                valid = band & (jj >= first_key)
                s = jnp.where(valid, s, -jnp.inf)
                mx = jnp.max(s, axis=-1, keepdims=True)
                lse = mx + jnp.log(jnp.sum(jnp.exp(s - mx), axis=-1, keepdims=True))
                p = jnp.exp(s - lse)
                o_ref[gi, 0, cur, :] = _dot(p.astype(BF16), vcat)
                lse_ref[gi, cur, :] = jnp.broadcast_to(lse, (blk, LANE))
                return carry

            lax.fori_loop(0, seq_len // blk, unit, 0)

    @pl.when(g == len(SWA_GROUPS) - 1)
    def _():
        lses = [lse_ref[gi] for gi in range(len(SWA_GROUPS))]
        mx = functools.reduce(jnp.maximum, lses)
        es = [jnp.exp(l - mx) for l in lses]
        den = functools.reduce(lambda a, b: a + b, es)
        for gi in range(len(SWA_GROUPS)):
            o_ref[gi, 0] = o_ref[gi, 0] * (es[gi] / den)


def swa_prompt(qkv_hm, cos_f, sin_f, *, batch, seq_len):
    ng = len(SWA_GROUPS)
    m = batch * seq_len
    qkv5 = qkv_hm.reshape(3, ng, SWA_HPG, m, LANE)
    o, kr = pl.pallas_call(
        functools.partial(_swa_prompt_kernel, seq_len=seq_len),
        out_shape=(jax.ShapeDtypeStruct((ng, SWA_HPG, m, LANE), F32),
                   jax.ShapeDtypeStruct((ng, SWA_HPG, m, LANE), F32)),
        grid=(batch, SWA_HPG, ng),
        in_specs=[
            pl.BlockSpec((3, 1, 1, seq_len, LANE), lambda b, h, g: (0, g, h, b, 0)),
            pl.BlockSpec((seq_len, LANE), lambda b, h, g: (0, 0)),
            pl.BlockSpec((seq_len, LANE), lambda b, h, g: (0, 0)),
        ],
        out_specs=(pl.BlockSpec((ng, 1, seq_len, LANE), lambda b, h, g: (0, h, b, 0)),
                   pl.BlockSpec((1, 1, seq_len, LANE), lambda b, h, g: (g, h, b, 0))),
        scratch_shapes=[pltpu.VMEM((seq_len, LANE), F32), pltpu.VMEM((ng, seq_len, LANE), F32)],
        compiler_params=_params("parallel", "parallel", "arbitrary"),
        name="swa_prompt",
    )(qkv5, cos_f, sin_f)
    return o.reshape(ng * SWA_HPG, m, LANE), kr


def _mem_prompt_kernel(q_ref, k_ref, v_ref, o_ref):
    s = _dot_nt(q_ref[0].astype(BF16), k_ref[0].astype(BF16)) * (LANE ** -0.5)
    mx = jnp.max(s, axis=-1, keepdims=True)
    e = jnp.exp(s - mx)
    p = e / jnp.sum(e, axis=-1, keepdims=True)
    o_ref[0] = _dot(p.astype(BF16), v_ref[0].astype(BF16))


def mem_prompt(q_hm, kv_hm, *, batch, seq_len, mem_len, tq):
    m = batch * seq_len
    nq = seq_len // tq
    return pl.pallas_call(
        _mem_prompt_kernel,
        out_shape=jax.ShapeDtypeStruct((MEM_HEADS, m, LANE), F32),
        grid=(batch, MEM_HEADS, nq),
        in_specs=[
            pl.BlockSpec((1, tq, LANE), lambda b, h, i: (h, b * nq + i, 0)),
            pl.BlockSpec((1, mem_len, LANE), lambda b, h, i: (h, b, 0)),
            pl.BlockSpec((1, mem_len, LANE), lambda b, h, i: (MEM_HEADS + h, b, 0)),
        ],
        out_specs=pl.BlockSpec((1, tq, LANE), lambda b, h, i: (h, b * nq + i, 0)),
        compiler_params=_params("parallel", "parallel", "parallel"),
        name="mem_prompt",
    )(q_hm, kv_hm, kv_hm)


def _mem_decode_kernel(q_ref, c_ref, o_ref):
    nh = q_ref.shape[0]
    mem_len = c_ref.shape[2] // (2 * nh)
    for h in range(nh):
        k = c_ref[0, 0, pl.ds(h, mem_len, stride=2 * nh), :].astype(BF16)
        v = c_ref[0, 0, pl.ds(nh + h, mem_len, stride=2 * nh), :].astype(BF16)
        s = _dot_nt(q_ref[h].astype(BF16), k) * (LANE ** -0.5)
        mx = jnp.max(s, axis=-1, keepdims=True)
        e = jnp.exp(s - mx)
        p = e / jnp.sum(e, axis=-1, keepdims=True)
        o_ref[h] = _dot(p.astype(BF16), v)


def mem_decode(q_hm, cache, layer, *, batch, steps):
    h = q_hm.shape[0]
    return pl.pallas_call(
        _mem_decode_kernel,
        out_shape=jax.ShapeDtypeStruct(q_hm.shape, F32),
        grid=(batch,),
        in_specs=[
            pl.BlockSpec((h, steps, LANE), lambda b: (0, b, 0)),
            pl.BlockSpec((1, 1, cache.shape[2], LANE), lambda b: (layer, b, 0, 0)),
        ],
        out_specs=pl.BlockSpec((h, steps, LANE), lambda b: (0, b, 0)),
        compiler_params=_params("parallel"),
        name="mem_decode",
    )(q_hm, cache)


def _swa_decode_kernel(qkv_ref, cos_ref, sin_ref, c0_ref, c1_ref, c2_ref,
                       o_ref, r0_ref, r1_ref, r2_ref, n0_ref, n1_ref, n2_ref, sem, *, steps):
    b = pl.program_id(0)
    ng = len(SWA_GROUPS)
    caches = (c0_ref, c1_ref, c2_ref)
    rolled = (r0_ref, r1_ref, r2_ref)
    staged = (n0_ref, n1_ref, n2_ref)
    rows_per_t = 2 * SWA_HPG
    scale = LANE ** -0.5

    def roll_copy(g):
        keep = caches[g].shape[1] - steps * rows_per_t
        return pltpu.make_async_copy(caches[g].at[0, pl.ds(steps * rows_per_t, keep)],
                                     rolled[g].at[b, pl.ds(0, keep)], sem.at[g, 0])

    def new_copy(g):
        keep = caches[g].shape[1] - steps * rows_per_t
        return pltpu.make_async_copy(staged[g], rolled[g].at[b, pl.ds(keep, steps * rows_per_t)], sem.at[g, 1])

    for g in range(ng):
        roll_copy(g).start()

    cos_f = cos_ref[...]
    sin_f = sin_ref[...]
    key_blk = lax.broadcasted_iota(jnp.int32, (SWA_SPAN, 1), 0)
    outs, lses = {}, {}
    for g, (window, dil) in enumerate(SWA_GROUPS):
        for h in range(SWA_HPG):
            hd = g * SWA_HPG + h
            q = _rope(qkv_ref[hd], cos_f, sin_f)
            k_new = _rope(qkv_ref[ng * SWA_HPG + hd], cos_f, sin_f)
            v_new = qkv_ref[2 * ng * SWA_HPG + hd]
            staged[g][pl.ds(2 * h, steps, stride=rows_per_t), :] = k_new
            staged[g][pl.ds(2 * h + 1, steps, stride=rows_per_t), :] = v_new
            o_rows = [None] * steps
            l_rows = [None] * steps
            for r in range(min(dil, steps)):
                k_row = (r * SWA_HPG + h) * 2
                k_c = caches[g][0, pl.ds(k_row, SWA_SPAN, stride=dil * rows_per_t), :]
                v_c = caches[g][0, pl.ds(k_row + 1, SWA_SPAN, stride=dil * rows_per_t), :]
                for s in range(r, steps, dil):
                    first = s // dil
                    q_s = q[s:s + 1, :]
                    sc = jnp.sum(k_c * q_s, axis=1, keepdims=True) * scale
                    if first > 0:
                        sc = jnp.where(key_blk >= first, sc, -jnp.inf)
                    new_rows = [m * dil + r for m in range(first + 1)]
                    sn = [jnp.sum(k_new[t:t + 1, :] * q_s, axis=1, keepdims=True) * scale for t in new_rows]
                    mx = jnp.max(sc, axis=0, keepdims=True)
                    for x in sn:
                        mx = jnp.maximum(mx, x)
                    e = jnp.exp(sc - mx)
                    den = jnp.sum(e, axis=0, keepdims=True)
                    acc = jnp.sum(e * v_c, axis=0, keepdims=True)
                    for x, t in zip(sn, new_rows):
                        ex = jnp.exp(x - mx)
                        den = den + ex
                        acc = acc + ex * v_new[t:t + 1, :]
                    o_rows[s] = acc / den
                    l_rows[s] = mx + jnp.log(den)
            outs[g, h] = jnp.concatenate(o_rows, axis=0)
            lses[g, h] = jnp.concatenate(l_rows, axis=0)

    for g in range(ng):
        new_copy(g).start()

    for h in range(SWA_HPG):
        mx = functools.reduce(jnp.maximum, [lses[g, h] for g in range(ng)])
        es = [jnp.exp(lses[g, h] - mx) for g in range(ng)]
        den = functools.reduce(lambda x, y: x + y, es)
        for g in range(ng):
            o_ref[g * SWA_HPG + h] = outs[g, h] * (es[g] / den)

    for g in range(ng):
        roll_copy(g).wait()
        new_copy(g).wait()


def swa_decode(qkv_hm, cos_f, sin_f, caches, *, batch, steps):
    ng = len(SWA_GROUPS)
    specs = []
    for (window, dil), c in zip(SWA_GROUPS, caches):
        assert c.shape[1] == window * SWA_HPG * 2 and window // dil == SWA_SPAN, "cache must hold exactly one window"
        assert steps <= dil or steps % dil == 0
        specs.append(pl.BlockSpec((1,) + c.shape[1:], lambda b: (b, 0, 0)))
    any_spec = pl.BlockSpec(memory_space=pl.ANY)
    out = pl.pallas_call(
        functools.partial(_swa_decode_kernel, steps=steps),
        out_shape=(jax.ShapeDtypeStruct((ng * SWA_HPG, batch * steps, LANE), F32),)
        + tuple(jax.ShapeDtypeStruct(c.shape, F32) for c in caches),
        grid=(batch,),
        in_specs=[
            pl.BlockSpec((3 * ng * SWA_HPG, steps, LANE), lambda b: (0, b, 0)),
            pl.BlockSpec((steps, LANE), lambda b: (0, 0)),
            pl.BlockSpec((steps, LANE), lambda b: (0, 0)),
        ] + specs,
        out_specs=(pl.BlockSpec((ng * SWA_HPG, steps, LANE), lambda b: (0, b, 0)),) + (any_spec,) * ng,
        scratch_shapes=[pltpu.VMEM((steps * SWA_HPG * 2, LANE), F32)] * ng + [pltpu.SemaphoreType.DMA((ng, 2))],
        compiler_params=_params("arbitrary"),
        name="swa_decode",
    )(qkv_hm, cos_f, sin_f, *caches)
    return out[0], out[1:]


def _rope_tables(pos):
    half = LANE // 2
    inv = ROPE_THETA ** (-jnp.arange(half, dtype=F32) / half)
    ang = pos.astype(F32)[:, None] * inv[None, :]
    cos, sin = jnp.cos(ang), jnp.sin(ang)
    return jnp.concatenate([cos, cos], axis=1), jnp.concatenate([-sin, sin], axis=1)


def _hm_to_tokens(x_hm, batch, seq_len):
    h = x_hm.shape[0]
    return x_hm.reshape(h, batch, seq_len, LANE).transpose(1, 2, 0, 3)


def _tile_m(m):
    return 1024 if m % 1024 == 0 else m


def kernel(x_prompt, x_sample, state_gdn, state_gdn_conv, cache_swa_kv0, cache_swa_kv1, cache_swa_kv2,
           cache_mem_kv, state_ffn_conv, mem_prompt_in, norm_gains, gdn_w_in, gdn_conv_w, gdn_a_log,
           gdn_dt_bias, gdn_norm, gdn_w_out, swa_w_qkv, swa_w_out, mem_w_q, mem_w_kv, mem_w_o,
           ffn_w_up, ffn_conv_w, ffn_conv_b, ffn_w_down):
    depth = norm_gains.shape[0]
    d_model = x_prompt.shape[-1]
    qk_dim = GDN_QK_HEADS * LANE
    v_dim = GDN_V_HEADS * LANE
    conv_dim = 2 * qk_dim + v_dim
    main_dim = conv_dim + v_dim
    ncols = conv_dim // LANE

    def gdn_weights(ia):
        w_in = gdn_w_in[ia]
        w_gate = jnp.zeros((d_model, 2 * LANE), F32)
        w_gate = w_gate.at[:, :GDN_V_HEADS].set(w_in[:, main_dim:main_dim + GDN_V_HEADS])
        w_gate = w_gate.at[:, LANE:LANE + GDN_V_HEADS].set(w_in[:, main_dim + GDN_V_HEADS:])
        cw = gdn_conv_w[ia].reshape(GDN_CONV, ncols, LANE).transpose(1, 0, 2)
        cw = jnp.pad(cw, ((0, 0), (0, SUBLANE - GDN_CONV), (0, 0)))
        gp = jnp.zeros((SUBLANE, LANE), F32)
        gp = gp.at[0, :GDN_V_HEADS].set(gdn_a_log[ia]).at[1, :GDN_V_HEADS].set(gdn_dt_bias[ia])
        return (w_in[:, :main_dim].astype(BF16), w_gate.astype(BF16), cw, gp, gdn_norm[ia],
                gdn_w_out[ia].astype(BF16))

    def run_trunk(x3, pos, prompt, mem_src):
        batch, seq_len, _ = x3.shape
        m = batch * seq_len
        x = x3.reshape(m, d_model)
        tm1 = _tile_m(m)
        tm2 = 512 if m % 512 == 0 else m
        cos_f, sin_f = _rope_tables(pos)
        out = {"gdn_s": [], "gdn_c": [], "swa": [[] for _ in SWA_GROUPS], "ffn": [], "mem_kv": []}
        ia = ib = 0
        for layer in range(depth):
            g = norm_gains[layer]
            if layer % 2 == 0:
                w_main, w_gate, cw, gp, o_gain, w_out = gdn_weights(ia)
                p = norm_matmul(x, g[0], w_main, tm=tm1, tn=512, head_major=True)
                gates = norm_matmul(x, g[0], w_gate, tm=tm1, tn=2 * LANE)
                if prompt:
                    tail0 = jnp.zeros((batch, ncols, SUBLANE, LANE), F32)
                    s0 = jnp.zeros((batch, GDN_V_HEADS, LANE, LANE), F32)
                else:
                    cb = state_gdn_conv[ia].reshape(batch, GDN_CONV - 1, ncols, LANE).transpose(0, 2, 1, 3)
                    tail0 = jnp.pad(cb, ((0, 0), (0, 0), (SUBLANE - (GDN_CONV - 1), 0), (0, 0)))
                    s0 = state_gdn[ia]
                o_hm, s1 = gdn_core(p, gates, cw, gp, o_gain, tail0, s0, batch=batch, seq_len=seq_len)
                out["gdn_s"].append(s1)
                c1 = p.reshape(-1, batch, seq_len, LANE)[:ncols, :, seq_len - (GDN_CONV - 1):]
                out["gdn_c"].append(c1.transpose(1, 2, 0, 3).reshape(batch, GDN_CONV - 1, conv_dim))
                x = matmul_resnorm(o_hm, w_out, x, g[1], tm=tm2, tk=1024, head_major=True)
                ia += 1
            else:
                w_qkv = swa_w_qkv[ib].astype(BF16)
                w_out = swa_w_out[ib].astype(BF16)
                ng = len(SWA_GROUPS)
                if prompt:
                    qkv_hm = norm_matmul(x, g[0], w_qkv, tm=tm1, tn=768, head_major=True)
                    o_hm, kr = swa_prompt(qkv_hm, cos_f, sin_f, batch=batch, seq_len=seq_len)
                    v_hm = qkv_hm.reshape(3, ng, SWA_HPG, m, LANE)[2]
                    for gi, (window, _) in enumerate(SWA_GROUPS):
                        keep = min(window, seq_len)
                        kv = jnp.stack([_hm_to_tokens(kr[gi], batch, seq_len),
                                        _hm_to_tokens(v_hm[gi], batch, seq_len)], axis=2)
                        out["swa"][gi].append(kv[:, seq_len - keep:])
                    x = matmul_resnorm(o_hm, w_out, x, g[1], tm=tm2, tk=768, head_major=True)
                else:
                    qkv_hm = norm_matmul(x, g[0], w_qkv, tm=tm1, tn=768, head_major=True)
                    caches = [c[ib].transpose(0, 1, 3, 2, 4).reshape(batch, -1, LANE)
                              for c in (cache_swa_kv0, cache_swa_kv1, cache_swa_kv2)]
                    o_hm, rolled = swa_decode(qkv_hm, cos_f, sin_f, caches, batch=batch, steps=seq_len)
                    for gi in range(ng):
                        r5 = rolled[gi].reshape(batch, -1, SWA_HPG, 2, LANE)
                        out["swa"][gi].append(r5.transpose(0, 1, 3, 2, 4))
                    x = matmul_resnorm(o_hm, w_out, x, g[1], tm=tm2, tk=768, head_major=True)
                ib += 1
            w_q = mem_w_q[layer].astype(BF16)
            w_o = mem_w_o[layer].astype(BF16)
            q_hm = norm_matmul(x, g[2], w_q, tm=tm1, tn=512, head_major=True)
            if prompt:
                mem_len = mem_src.shape[1]
                kv_hm = norm_matmul(mem_src.reshape(batch * mem_len, d_model), g[2], mem_w_kv[layer].astype(BF16),
                                    tm=_tile_m(batch * mem_len), tn=512, do_norm=False, head_major=True)
                out["mem_kv"].append(kv_hm.reshape(2, MEM_HEADS, batch, mem_len, LANE).transpose(2, 3, 0, 1, 4))
                o_hm = mem_prompt(q_hm, kv_hm, batch=batch, seq_len=seq_len, mem_len=mem_len, tq=1024)
            else:
                mem_rows = mem_src.reshape(mem_src.shape[0], batch, -1, LANE)
                o_hm = mem_decode(q_hm, mem_rows, layer, batch=batch, steps=seq_len)
            x = matmul_resnorm(o_hm, w_o, x, g[3], tm=tm2, tk=512, head_major=True)
            w_up = ffn_w_up[layer].astype(BF16)
            w_down = ffn_w_down[layer].astype(BF16)
            if prompt:
                act, gt = ffn_up(x, g[4], w_up, ffn_conv_w[layer], ffn_conv_b[layer], seq_len=seq_len, tm=tm1, tn=512)
                per_seq = seq_len // tm1
                out["ffn"].append(gt[per_seq - 1::per_seq, SUBLANE - (FFN_CONV - 1):])
            else:
                buf = state_ffn_conv[layer]
                zero = jnp.zeros((batch, seq_len - 1, buf.shape[-1]), F32)
                hist1 = jnp.concatenate([buf[:, 1:2], zero], axis=1).reshape(m, -1)
                hist2 = jnp.concatenate([buf, zero[:, 1:]], axis=1).reshape(m, -1)
                act, gt = ffn_up(x, g[4], w_up, ffn_conv_w[layer], ffn_conv_b[layer], seq_len=seq_len, tm=tm1,
                                 tn=512, hist=(hist1, hist2))
                out["ffn"].append(gt.reshape(batch, seq_len, -1)[:, seq_len - (FFN_CONV - 1):])
            x = matmul_resnorm(act, w_down, x, g[5], tm=tm2, tk=1408, head_major=False)
        return x.reshape(batch, seq_len, d_model), out

    pos_p = jnp.arange(x_prompt.shape[1])
    y_p, op = run_trunk(x_prompt, pos_p, True, mem_prompt_in)
    pos_s = PAST_LEN + jnp.arange(x_sample.shape[1])
    y_s, os_ = run_trunk(x_sample, pos_s, False, cache_mem_kv)

    return (y_p, y_s, jnp.stack(op["gdn_s"]), jnp.stack(op["gdn_c"]),
            jnp.stack(op["swa"][0]), jnp.stack(op["swa"][1]), jnp.stack(op["swa"][2]),
            jnp.stack(op["mem_kv"]), jnp.stack(op["ffn"]),
            jnp.stack(os_["gdn_s"]), jnp.stack(os_["gdn_c"]),
            jnp.stack(os_["swa"][0]), jnp.stack(os_["swa"][1]), jnp.stack(os_["swa"][2]),
            jnp.stack(os_["ffn"]))
```

```python
import functools
import math

import jax
import jax.numpy as jnp
from jax import lax
from jax.experimental import pallas as pl
from jax.experimental.pallas import tpu as pltpu

F32 = jnp.float32
BF16 = jnp.bfloat16
HIGHEST = lax.Precision.HIGHEST

LANE = 128
SUBLANE = 8
NORM_EPS = 1e-6
ROPE_THETA = 10000.0
PAST_LEN = 2048
GDN_CHUNK = 64
GDN_QK_HEADS = 16
GDN_V_HEADS = 32
GDN_CONV = 4
SWA_GROUPS = ((128, 1), (512, 4), (2048, 16))
SWA_HPG = 6
SWA_SPAN = 128
MEM_HEADS = 4
FFN_CONV = 3
VMEM_LIMIT_BYTES = 52 * 1024 * 1024
MRN_COL_CHUNK = 512


def _params(*semantics):
    return pltpu.CompilerParams(dimension_semantics=semantics, vmem_limit_bytes=VMEM_LIMIT_BYTES)


def _dot(a, b):
    return jnp.dot(a, b, preferred_element_type=F32)


def _dot_nt(a, b):
    return lax.dot_general(a, b, (((1,), (1,)), ((), ())), preferred_element_type=F32)


def _dot_tn(a, b):
    return lax.dot_general(a, b, (((0,), (0,)), ((), ())), preferred_element_type=F32)


def _rms(x, gain):
    return x * lax.rsqrt(jnp.mean(x * x, axis=-1, keepdims=True) + NORM_EPS) * gain


def _silu(x):
    return x * jax.nn.sigmoid(x)


def _nmm_kernel(x_ref, g_ref, w_ref, o_ref, h_ref, *, do_norm, head_major):
    @pl.when(pl.program_id(1) == 0)
    def _():
        x = x_ref[...]
        if do_norm:
            x = _rms(x, g_ref[...])
        h_ref[...] = x.astype(BF16)

    acc = _dot(h_ref[...], w_ref[0].astype(BF16))
    if head_major:
        for c in range(o_ref.shape[0]):
            o_ref[c] = acc[:, c * LANE:(c + 1) * LANE]
    else:
        o_ref[...] = acc


def norm_matmul(x, gain, w, layer, *, tm, tn, n=None, do_norm=True, head_major=False):
    m, k = x.shape
    n = w.shape[2] if n is None else n
    assert m % tm == 0 and n % tn == 0 and tn % LANE == 0
    if head_major:
        out_shape = jax.ShapeDtypeStruct((n // LANE, m, LANE), F32)
        out_spec = pl.BlockSpec((tn // LANE, tm, LANE), lambda i, j: (j, i, 0))
    else:
        out_shape = jax.ShapeDtypeStruct((m, n), F32)
        out_spec = pl.BlockSpec((tm, tn), lambda i, j: (i, j))
    return pl.pallas_call(
        functools.partial(_nmm_kernel, do_norm=do_norm, head_major=head_major),
        out_shape=out_shape,
        grid=(m // tm, n // tn),
        in_specs=[
            pl.BlockSpec((tm, k), lambda i, j: (i, 0)),
            pl.BlockSpec((1, k), lambda i, j: (0, 0)),
            pl.BlockSpec((1, k, tn), lambda i, j: (layer, 0, j)),
        ],
        out_specs=out_spec,
        scratch_shapes=[pltpu.VMEM((tm, k), BF16)],
        compiler_params=_params("parallel", "arbitrary"),
        name="norm_matmul",
    )(x, gain.reshape(1, k), w)


def _mrn_kernel(a_ref, w_ref, x_ref, g_ref, o_ref, *, head_major):
    k = pl.program_id(1)
    if head_major:
        a = jnp.concatenate([a_ref[c].astype(BF16) for c in range(a_ref.shape[0])], axis=1)
    else:
        a = a_ref[...].astype(BF16)
    d = o_ref.shape[1]
    cols = [slice(c, c + MRN_COL_CHUNK) for c in range(0, d, MRN_COL_CHUNK)]
    @pl.when(k == 0)
    def _():
        o_ref[...] = jnp.zeros_like(o_ref)

    for cs in cols:
        o_ref[:, cs] += _dot(a, w_ref[0, :, cs])

    @pl.when(k == pl.num_programs(1) - 1)
    def _():
        ss = functools.reduce(lambda x, y: x + y,
                              [jnp.sum(o_ref[:, cs] * o_ref[:, cs], axis=-1, keepdims=True) for cs in cols])
        inv = lax.rsqrt(ss * (1.0 / d) + NORM_EPS)
        for cs in cols:
            o_ref[:, cs] = x_ref[:, cs] + (o_ref[:, cs] * inv) * g_ref[:, cs]


def matmul_resnorm(a, w, layer, x, gain, *, tm, tk, head_major):
    m, d = x.shape
    kdim = w.shape[1]
    assert m % tm == 0 and kdim % tk == 0 and tk % LANE == 0
    if head_major:
        a_spec = pl.BlockSpec((tk // LANE, tm, LANE), lambda i, k: (k, i, 0))
    else:
        a_spec = pl.BlockSpec((tm, tk), lambda i, k: (i, k))
    return pl.pallas_call(
        functools.partial(_mrn_kernel, head_major=head_major),
        out_shape=jax.ShapeDtypeStruct((m, d), F32),
        grid=(m // tm, kdim // tk),
        in_specs=[
            a_spec,
            pl.BlockSpec((1, tk, d), lambda i, k: (layer, k, 0)),
            pl.BlockSpec((tm, d), lambda i, k: (i, 0)),
            pl.BlockSpec((1, d), lambda i, k: (0, 0)),
        ],
        out_specs=pl.BlockSpec((tm, d), lambda i, k: (i, 0)),
        compiler_params=_params("parallel", "arbitrary"),
        name="matmul_resnorm",
    )(a, w, x, gain.reshape(1, d))


def _gelu_tanh(x):
    c = math.sqrt(2.0 / math.pi)
    return x * (0.5 * (1.0 + jnp.tanh(c * (x + 0.044715 * (x * x * x)))))


def _ffn_up_kernel(*refs, seq_len, tm, long_seq):
    if long_seq:
        x_ref, g_ref, wg_ref, wu_ref, cw_ref, cb_ref, act_ref, gt_ref, h_ref, carry_ref = refs
    else:
        x_ref, g_ref, wg_ref, wu_ref, cw_ref, cb_ref, h1_ref, h2_ref, act_ref, gt_ref, h_ref = refs
    i = pl.program_id(0)
    j = pl.program_id(1)

    @pl.when(j == 0)
    def _():
        h_ref[...] = _rms(x_ref[...], g_ref[...]).astype(BF16)

    h = h_ref[...]
    gate = _dot(h, wg_ref[0].astype(BF16))
    up = _dot(h, wu_ref[0].astype(BF16))
    row = lax.broadcasted_iota(jnp.int32, gate.shape, 0)
    sh1 = pltpu.roll(gate, 1, axis=0)
    sh2 = pltpu.roll(gate, 2, axis=0)
    if long_seq:
        prev = jnp.where((i * tm) % seq_len == 0, 0.0, carry_ref[j])
        carry_ref[j] = gate[tm - SUBLANE:tm]
        gt_ref[0] = gate[tm - SUBLANE:tm]
        sh1 = jnp.where(row == 0, prev[7:8], sh1)
        sh2 = jnp.where(row == 0, prev[6:7], jnp.where(row == 1, prev[7:8], sh2))
    else:
        pos = row % SUBLANE
        gt_ref[...] = gate
        sh1 = jnp.where(pos == 0, 0.0, sh1) + h1_ref[...]
        sh2 = jnp.where(pos < 2, 0.0, sh2) + h2_ref[...]
    y = sh2 * cw_ref[0:1] + sh1 * cw_ref[1:2] + gate * cw_ref[2:3]
    act_ref[...] = (_gelu_tanh(y + cb_ref[...]) * up).astype(BF16)


def ffn_up(x, gain, w_up, layer, conv_w, conv_b, *, seq_len, tm, tn, hist=None):
    m, k = x.shape
    f = conv_w.shape[1]
    nj = f // tn
    assert m % tm == 0 and f % tn == 0
    long_seq = seq_len >= tm
    if long_seq:
        assert seq_len % tm == 0
    else:
        assert seq_len == SUBLANE
    in_specs = [
        pl.BlockSpec((tm, k), lambda i, j: (i, 0)),
        pl.BlockSpec((1, k), lambda i, j: (0, 0)),
        pl.BlockSpec((1, k, tn), lambda i, j: (layer, 0, j)),
        pl.BlockSpec((1, k, tn), lambda i, j: (layer, 0, j + nj)),
        pl.BlockSpec((FFN_CONV, tn), lambda i, j: (0, j)),
        pl.BlockSpec((1, tn), lambda i, j: (0, j)),
    ]
    args = [x, gain.reshape(1, k), w_up, w_up, conv_w, conv_b.reshape(1, f)]
    scratch = [pltpu.VMEM((tm, k), BF16)]
    if long_seq:
        gt_shape = jax.ShapeDtypeStruct((m // tm, SUBLANE, f), F32)
        gt_spec = pl.BlockSpec((1, SUBLANE, tn), lambda i, j: (i, 0, j))
        scratch.append(pltpu.VMEM((nj, SUBLANE, tn), F32))
    else:
        gt_shape = jax.ShapeDtypeStruct((m, f), F32)
        gt_spec = pl.BlockSpec((tm, tn), lambda i, j: (i, j))
        in_specs += [pl.BlockSpec((tm, tn), lambda i, j: (i, j))] * 2
        args += list(hist)
    return pl.pallas_call(
        functools.partial(_ffn_up_kernel, seq_len=seq_len, tm=tm, long_seq=long_seq),
        out_shape=(jax.ShapeDtypeStruct((m, f), BF16), gt_shape),
        grid=(m // tm, nj),
        in_specs=in_specs,
        out_specs=(pl.BlockSpec((tm, tn), lambda i, j: (i, j)), gt_spec),
        scratch_shapes=scratch,
        compiler_params=_params("arbitrary", "arbitrary"),
        name="ffn_up",
    )(*args)


def _gdn_kernel(p_ref, g_ref, cw_ref, gp_ref, gain_ref, tail0_ref, s0_ref, o_ref, s_ref,
                tail_ref, gct_ref, *, chunk, heads_per_group):
    c = chunk
    nqk = GDN_QK_HEADS
    rep = GDN_V_HEADS // GDN_QK_HEADS

    @pl.when(pl.program_id(1) == 0)
    def _():
        s_ref[...] = s0_ref[...]
        tail_ref[...] = tail0_ref[0]

    gates = g_ref[...]
    a_log = gp_ref[0:1, :]
    dt_bias = gp_ref[1:2, :]
    beta_all = jax.nn.sigmoid(gates[:, :LANE])
    xa = gates[:, LANE:] + dt_bias
    softplus = jnp.maximum(xa, 0.0) + jnp.log1p(jnp.exp(-jnp.abs(xa)))
    g_all = -jnp.exp(a_log) * softplus
    row = lax.broadcasted_iota(jnp.int32, (c, c), 0)
    col = lax.broadcasted_iota(jnp.int32, (c, c), 1)
    causal = row >= col
    strict = row > col
    eye = (row == col).astype(F32)
    gc_all = jnp.dot(causal.astype(F32), g_all, precision=HIGHEST, preferred_element_type=F32)
    gct_ref[...] = gc_all.T
    gain = gain_ref[...]

    def conv_silu(cb):
        x = p_ref[cb]
        w = cw_ref[cb]
        ext = jnp.concatenate([tail_ref[cb], x], axis=0)
        y = ext[5:5 + c] * w[0:1] + ext[6:6 + c] * w[1:2] + ext[7:7 + c] * w[2:3] + x * w[3:4]
        tail_ref[cb] = x[c - SUBLANE:c]
        return _silu(y)

    def l2norm(x):
        return x * lax.rsqrt(jnp.sum(x * x, axis=-1, keepdims=True) + NORM_EPS)

    def head_group(qk_heads):
        v_heads = [rep * j + e for j in qk_heads for e in range(rep)]
        q, k, kk, qk = {}, {}, {}, {}
        for j in qk_heads:
            q[j] = l2norm(conv_silu(j)) * (LANE ** -0.5)
            k[j] = l2norm(conv_silu(nqk + j))
        for j in qk_heads:
            k16 = k[j].astype(BF16)
            kk[j] = _dot_nt(k16, k16)
            qk[j] = _dot_nt(q[j].astype(BF16), k16)
        beta_c, gc_c, a_intra, xm, tinv = {}, {}, {}, {}, {}
        for hv in v_heads:
            j = hv // rep
            beta_c[hv] = beta_all[:, hv:hv + 1]
            gc_c[hv] = gc_all[:, hv:hv + 1]
            diff = gc_c[hv] - gct_ref[hv:hv + 1, :]
            decay = jnp.where(causal, jnp.exp(jnp.where(causal, diff, 0.0)), 0.0)
            a_intra[hv] = (qk[j] * decay).astype(BF16)
            xm[hv] = -jnp.where(strict, (beta_c[hv] * kk[j]) * decay, 0.0)
            tinv[hv] = eye + xm[hv]
        for _ in range(int(math.log2(c)) - 1):
            for hv in v_heads:
                xm16 = xm[hv].astype(BF16)
                xm[hv] = _dot(xm16, xm16)
            for hv in v_heads:
                tinv[hv] = tinv[hv] + _dot(tinv[hv].astype(BF16), xm[hv].astype(BF16))
        u, w = {}, {}
        for hv in v_heads:
            j = hv // rep
            tinv16 = tinv[hv].astype(BF16)
            v = conv_silu(2 * nqk + hv)
            u[hv] = _dot(tinv16, (v * beta_c[hv]).astype(BF16))
            w[hv] = _dot(tinv16, (k[j] * (beta_c[hv] * jnp.exp(gc_c[hv]))).astype(BF16))
        v_new16, o_inter = {}, {}
        for hv in v_heads:
            j = hv // rep
            s16 = s_ref[0, hv].astype(BF16)
            v_new16[hv] = (u[hv] - _dot(w[hv].astype(BF16), s16)).astype(BF16)
            o_inter[hv] = _dot((q[j] * jnp.exp(gc_c[hv])).astype(BF16), s16)
        for hv in v_heads:
            j = hv // rep
            gl = gc_c[hv][c - 1:c, :]
            o = o_inter[hv] + _dot(a_intra[hv], v_new16[hv])
            k_dec = k[j] * jnp.exp(gl - gc_c[hv])
            s_ref[0, hv] = s_ref[0, hv] * jnp.exp(gl) + _dot_tn(k_dec.astype(BF16), v_new16[hv])
            z = p_ref[2 * nqk + GDN_V_HEADS + hv]
            o_ref[hv] = (_rms(o, gain) * _silu(z)).astype(o_ref.dtype)

    for j0 in range(0, nqk, heads_per_group):
        head_group(range(j0, j0 + heads_per_group))


def gdn_core(p, gates, conv_w_hm, gate_params, o_gain, tail0, state0, *, batch, seq_len, heads_per_group=8):
    nblk, m, _ = p.shape
    c = min(GDN_CHUNK, seq_len)
    assert seq_len % c == 0 and c % SUBLANE == 0 and m == batch * seq_len
    nc = seq_len // c
    ncols = 2 * GDN_QK_HEADS + GDN_V_HEADS
    o_dtype = BF16 if c % (2 * SUBLANE) == 0 else F32
    return pl.pallas_call(
        functools.partial(_gdn_kernel, chunk=c, heads_per_group=heads_per_group),
        out_shape=(jax.ShapeDtypeStruct((GDN_V_HEADS, m, LANE), o_dtype),
                   jax.ShapeDtypeStruct((batch, GDN_V_HEADS, LANE, LANE), F32)),
        grid=(batch, nc),
        in_specs=[
            pl.BlockSpec((nblk, c, LANE), lambda b, n: (0, b * nc + n, 0)),
            pl.BlockSpec((c, 2 * LANE), lambda b, n: (b * nc + n, 0)),
            pl.BlockSpec((ncols, SUBLANE, LANE), lambda b, n: (0, 0, 0)),
            pl.BlockSpec((SUBLANE, LANE), lambda b, n: (0, 0)),
            pl.BlockSpec((1, LANE), lambda b, n: (0, 0)),
            pl.BlockSpec((1, ncols, SUBLANE, LANE), lambda b, n: (b, 0, 0, 0)),
            pl.BlockSpec((1, GDN_V_HEADS, LANE, LANE), lambda b, n: (b, 0, 0, 0)),
        ],
        out_specs=(pl.BlockSpec((GDN_V_HEADS, c, LANE), lambda b, n: (0, b * nc + n, 0)),
                   pl.BlockSpec((1, GDN_V_HEADS, LANE, LANE), lambda b, n: (b, 0, 0, 0))),
        scratch_shapes=[pltpu.VMEM((ncols, SUBLANE, LANE), F32), pltpu.VMEM((LANE, c), F32)],
        compiler_params=_params("parallel", "arbitrary"),
        name="gdn_core",
    )(p, gates, conv_w_hm, gate_params, o_gain.reshape(1, LANE), tail0, state0)


def _rope(x, cos_f, sin_f):
    return x * cos_f + pltpu.roll(x, LANE // 2, axis=1) * sin_f


def _swa_prompt_kernel(qkv_ref, cos_ref, sin_ref, o_ref, kr_ref, qs_ref, lse_ref, *, seq_len):
    g = pl.program_id(2)
    cos_f = cos_ref[...]
    sin_f = sin_ref[...]
    qs_ref[...] = _rope(qkv_ref[0, 0, 0], cos_f, sin_f)
    kr_ref[0, 0] = _rope(qkv_ref[1, 0, 0], cos_f, sin_f)
    blk = SWA_SPAN
    ii = lax.broadcasted_iota(jnp.int32, (blk, 2 * blk), 0)
    jj = lax.broadcasted_iota(jnp.int32, (blk, 2 * blk), 1)
    dist = ii + blk - jj
    band = (dist >= 0) & (dist <= SWA_SPAN)
    scale = LANE ** -0.5
    units_per_iter = 4
    assert (seq_len // blk) % units_per_iter == 0

    for gi, (window, dil) in enumerate(SWA_GROUPS):
        assert window // dil == SWA_SPAN and seq_len % (blk * dil) == 0
        nblk = seq_len // dil // blk

        @pl.when(g == gi)
        def _(gi=gi, dil=dil, nblk=nblk):
            def units(it, carry):
                cur, first_key, q, kcat, vcat = [], [], [], [], []
                for i in range(units_per_iter):
                    u = it * units_per_iter + i
                    r = u // nblk
                    b = u % nblk
                    cur.append(pl.ds(r + b * blk * dil, blk, stride=dil))
                    prv = pl.ds(r + jnp.maximum(b - 1, 0) * blk * dil, blk, stride=dil)
                    first_key.append(jnp.where(b > 0, 0, blk))
                    q.append(qs_ref[cur[i], :].astype(BF16))
                    kcat.append(jnp.concatenate([kr_ref[0, 0, prv, :], kr_ref[0, 0, cur[i], :]], axis=0).astype(BF16))
                    vcat.append(jnp.concatenate([qkv_ref[2, 0, 0, prv, :], qkv_ref[2, 0, 0, cur[i], :]],
                                                axis=0).astype(BF16))
                s = [_dot_nt(q[i], kcat[i]) * scale for i in range(units_per_iter)]
                p = []
                for i in range(units_per_iter):
                    si = jnp.where(band & (jj >= first_key[i]), s[i], -jnp.inf)
                    mx = jnp.max(si, axis=-1, keepdims=True)
                    e = jnp.exp(si - mx)
                    den = jnp.sum(e, axis=-1, keepdims=True)
                    p.append((e * (1.0 / den)).astype(BF16))
                    lse_ref[gi, cur[i], :] = jnp.broadcast_to(mx + jnp.log(den), (blk, LANE))
                for i in range(units_per_iter):
                    o_ref[gi, 0, cur[i], :] = _dot(p[i], vcat[i])
                return carry

            lax.fori_loop(0, seq_len // blk // units_per_iter, units, 0)

    @pl.when(g == len(SWA_GROUPS) - 1)
    def _():
        lses = [lse_ref[gi] for gi in range(len(SWA_GROUPS))]
        mx = functools.reduce(jnp.maximum, lses)
        es = [jnp.exp(l - mx) for l in lses]
        den = functools.reduce(lambda a, b: a + b, es)
        for gi in range(len(SWA_GROUPS)):
            o_ref[gi, 0] = o_ref[gi, 0] * (es[gi] / den)


def swa_prompt(qkv_hm, cos_f, sin_f, *, batch, seq_len):
    ng = len(SWA_GROUPS)
    m = batch * seq_len
    qkv5 = qkv_hm.reshape(3, ng, SWA_HPG, m, LANE)
    o, kr = pl.pallas_call(
        functools.partial(_swa_prompt_kernel, seq_len=seq_len),
        out_shape=(jax.ShapeDtypeStruct((ng, SWA_HPG, m, LANE), F32),
                   jax.ShapeDtypeStruct((ng, SWA_HPG, m, LANE), F32)),
        grid=(batch, SWA_HPG, ng),
        in_specs=[
            pl.BlockSpec((3, 1, 1, seq_len, LANE), lambda b, h, g: (0, g, h, b, 0)),
            pl.BlockSpec((seq_len, LANE), lambda b, h, g: (0, 0)),
            pl.BlockSpec((seq_len, LANE), lambda b, h, g: (0, 0)),
        ],
        out_specs=(pl.BlockSpec((ng, 1, seq_len, LANE), lambda b, h, g: (0, h, b, 0)),
                   pl.BlockSpec((1, 1, seq_len, LANE), lambda b, h, g: (g, h, b, 0))),
        scratch_shapes=[pltpu.VMEM((seq_len, LANE), F32), pltpu.VMEM((ng, seq_len, LANE), F32)],
        compiler_params=_params("parallel", "parallel", "arbitrary"),
        name="swa_prompt",
    )(qkv5, cos_f, sin_f)
    return o.reshape(ng * SWA_HPG, m, LANE), kr


def _mem_prompt_kernel(q_ref, k_ref, v_ref, o_ref):
    s = _dot_nt(q_ref[0].astype(BF16), k_ref[0].astype(BF16)) * (LANE ** -0.5)
    mx = jnp.max(s, axis=-1, keepdims=True)
    e = jnp.exp(s - mx)
    p = e / jnp.sum(e, axis=-1, keepdims=True)
    o_ref[0] = _dot(p.astype(BF16), v_ref[0].astype(BF16))


def mem_prompt(q_hm, kv_hm, *, batch, seq_len, mem_len, tq):
    m = batch * seq_len
    nq = seq_len // tq
    return pl.pallas_call(
        _mem_prompt_kernel,
        out_shape=jax.ShapeDtypeStruct((MEM_HEADS, m, LANE), F32),
        grid=(batch, MEM_HEADS, nq),
        in_specs=[
            pl.BlockSpec((1, tq, LANE), lambda b, h, i: (h, b * nq + i, 0)),
            pl.BlockSpec((1, mem_len, LANE), lambda b, h, i: (h, b, 0)),
            pl.BlockSpec((1, mem_len, LANE), lambda b, h, i: (MEM_HEADS + h, b, 0)),
        ],
        out_specs=pl.BlockSpec((1, tq, LANE), lambda b, h, i: (h, b * nq + i, 0)),
        compiler_params=_params("parallel", "parallel", "parallel"),
        name="mem_prompt",
    )(q_hm, kv_hm, kv_hm)


def _mem_decode_kernel(q_ref, c_ref, o_ref):
    nh = q_ref.shape[0]
    mem_len = c_ref.shape[2] // (2 * nh)
    for h in range(nh):
        k = c_ref[0, 0, pl.ds(h, mem_len, stride=2 * nh), :].astype(BF16)
        v = c_ref[0, 0, pl.ds(nh + h, mem_len, stride=2 * nh), :].astype(BF16)
        s = _dot_nt(q_ref[h].astype(BF16), k) * (LANE ** -0.5)
        mx = jnp.max(s, axis=-1, keepdims=True)
        e = jnp.exp(s - mx)
        p = e / jnp.sum(e, axis=-1, keepdims=True)
        o_ref[h] = _dot(p.astype(BF16), v)


def mem_decode(q_hm, cache, layer, *, batch, steps):
    h = q_hm.shape[0]
    return pl.pallas_call(
        _mem_decode_kernel,
        out_shape=jax.ShapeDtypeStruct(q_hm.shape, F32),
        grid=(batch,),
        in_specs=[
            pl.BlockSpec((h, steps, LANE), lambda b: (0, b, 0)),
            pl.BlockSpec((1, 1, cache.shape[2], LANE), lambda b: (layer, b, 0, 0)),
        ],
        out_specs=pl.BlockSpec((h, steps, LANE), lambda b: (0, b, 0)),
        compiler_params=_params("parallel"),
        name="mem_decode",
    )(q_hm, cache)


def _swa_decode_kernel(qkv_ref, cos_ref, sin_ref, c0_ref, c1_ref, c2_ref,
                       o_ref, r0_ref, r1_ref, r2_ref, n0_ref, n1_ref, n2_ref, sem, *, steps):
    b = pl.program_id(0)
    ng = len(SWA_GROUPS)
    caches = (c0_ref, c1_ref, c2_ref)
    rolled = (r0_ref, r1_ref, r2_ref)
    staged = (n0_ref, n1_ref, n2_ref)
    rows_per_t = 2 * SWA_HPG
    scale = LANE ** -0.5

    def roll_copy(g):
        keep = caches[g].shape[1] - 1
        return pltpu.make_async_copy(caches[g].at[0, pl.ds(1, keep)], rolled[g].at[b, pl.ds(0, keep)], sem.at[g, 0])

    def new_copy(g):
        return pltpu.make_async_copy(staged[g], rolled[g].at[b, caches[g].shape[1] - 1], sem.at[g, 1])

    for g in range(ng):
        roll_copy(g).start()

    cos_f = cos_ref[...]
    sin_f = sin_ref[...]
    u_col = lax.broadcasted_iota(jnp.int32, (steps, 1), 0)
    outs, lses = {}, {}
    for g, (window, dil) in enumerate(SWA_GROUPS):
        chunk_stride = max(dil // steps, 1)
        n_chunks = caches[g].shape[1] // chunk_stride
        shifts = list(range(0, steps, dil)) if dil < steps else [0]
        c_idx = lax.broadcasted_iota(jnp.int32, (n_chunks, steps, 1), 0)
        u_idx = lax.broadcasted_iota(jnp.int32, (n_chunks, steps, 1), 1)
        tau = c_idx * (steps * chunk_stride) + u_idx
        for h in range(SWA_HPG):
            hd = g * SWA_HPG + h
            q = _rope(qkv_ref[hd], cos_f, sin_f)
            k_new = _rope(qkv_ref[ng * SWA_HPG + hd], cos_f, sin_f)
            v_new = qkv_ref[2 * ng * SWA_HPG + hd]
            staged[g][pl.ds(2 * h, steps, stride=rows_per_t), :] = k_new
            staged[g][pl.ds(2 * h + 1, steps, stride=rows_per_t), :] = v_new
            k_c = caches[g][0, pl.ds(0, n_chunks, stride=chunk_stride), pl.ds(2 * h, steps, stride=rows_per_t), :]
            v_c = caches[g][0, pl.ds(0, n_chunks, stride=chunk_stride), pl.ds(2 * h + 1, steps, stride=rows_per_t), :]
            parts = []
            for a in shifts:
                q_a = q if a == 0 else pltpu.roll(q, steps - a, axis=0)
                sc = jnp.sum(k_c * q_a[None], axis=-1, keepdims=True) * scale
                sn = jnp.sum(k_new * q_a, axis=-1, keepdims=True) * scale
                if a > 0:
                    s_idx = (u_idx + a) & (steps - 1)
                    sc = jnp.where(tau >= s_idx, sc, -jnp.inf)
                    sn = jnp.where(u_col + a < steps, sn, -jnp.inf)
                mx = jnp.maximum(jnp.max(sc, axis=0), sn)
                e = jnp.exp(sc - mx[None])
                en = jnp.exp(sn - mx)
                den = jnp.sum(e, axis=0) + en
                acc = jnp.sum(e * v_c, axis=0) + en * v_new
                mx = jnp.broadcast_to(mx, (steps, LANE))
                den = jnp.broadcast_to(den, (steps, LANE))
                if a > 0:
                    mx, den, acc = (pltpu.roll(x, a, axis=0) for x in (mx, den, acc))
                parts.append((mx, den, acc))
            mx = functools.reduce(jnp.maximum, [p[0] for p in parts])
            ws = [jnp.exp(p[0] - mx) for p in parts]
            den = functools.reduce(lambda x, y: x + y, [w * p[1] for w, p in zip(ws, parts)])
            acc = functools.reduce(lambda x, y: x + y, [w * p[2] for w, p in zip(ws, parts)])
            outs[g, h] = acc / den
            lses[g, h] = mx + jnp.log(den)

    for g in range(ng):
        new_copy(g).start()

    for h in range(SWA_HPG):
        mx = functools.reduce(jnp.maximum, [lses[g, h] for g in range(ng)])
        es = [jnp.exp(lses[g, h] - mx) for g in range(ng)]
        den = functools.reduce(lambda x, y: x + y, es)
        for g in range(ng):
            o_ref[g * SWA_HPG + h] = outs[g, h] * (es[g] / den)

    for g in range(ng):
        roll_copy(g).wait()
        new_copy(g).wait()


def swa_decode(qkv_hm, cos_f, sin_f, caches, *, batch, steps):
    ng = len(SWA_GROUPS)
    assert steps & (steps - 1) == 0 and steps % SUBLANE == 0
    specs = []
    for (window, dil), c in zip(SWA_GROUPS, caches):
        assert c.shape[1] * steps == window and window // dil == SWA_SPAN, "cache must hold exactly one window"
        assert dil % steps == 0 or steps % dil == 0
        specs.append(pl.BlockSpec((1,) + c.shape[1:], lambda b: (b, 0, 0, 0)))
    any_spec = pl.BlockSpec(memory_space=pl.ANY)
    out = pl.pallas_call(
        functools.partial(_swa_decode_kernel, steps=steps),
        out_shape=(jax.ShapeDtypeStruct((ng * SWA_HPG, batch * steps, LANE), F32),)
        + tuple(jax.ShapeDtypeStruct(c.shape, F32) for c in caches),
        grid=(batch,),
        in_specs=[
            pl.BlockSpec((3 * ng * SWA_HPG, steps, LANE), lambda b: (0, b, 0)),
            pl.BlockSpec((steps, LANE), lambda b: (0, 0)),
            pl.BlockSpec((steps, LANE), lambda b: (0, 0)),
        ] + specs,
        out_specs=(pl.BlockSpec((ng * SWA_HPG, steps, LANE), lambda b: (0, b, 0)),) + (any_spec,) * ng,
        scratch_shapes=[pltpu.VMEM((steps * SWA_HPG * 2, LANE), F32)] * ng + [pltpu.SemaphoreType.DMA((ng, 2))],
        compiler_params=_params("arbitrary"),
        name="swa_decode",
    )(qkv_hm, cos_f, sin_f, *caches)
    return out[0], out[1:]


def _rope_tables(pos):
    half = LANE // 2
    inv = ROPE_THETA ** (-jnp.arange(half, dtype=F32) / half)
    ang = pos.astype(F32)[:, None] * inv[None, :]
    cos, sin = jnp.cos(ang), jnp.sin(ang)
    return jnp.concatenate([cos, cos], axis=1), jnp.concatenate([-sin, sin], axis=1)


def _hm_to_tokens(x_hm, batch, seq_len):
    h = x_hm.shape[0]
    return x_hm.reshape(h, batch, seq_len, LANE).transpose(1, 2, 0, 3)


def _tile_m(m):
    return 1024 if m % 1024 == 0 else m


def kernel(x_prompt, x_sample, state_gdn, state_gdn_conv, cache_swa_kv0, cache_swa_kv1, cache_swa_kv2,
           cache_mem_kv, state_ffn_conv, mem_prompt_in, norm_gains, gdn_w_in, gdn_conv_w, gdn_a_log,
           gdn_dt_bias, gdn_norm, gdn_w_out, swa_w_qkv, swa_w_out, mem_w_q, mem_w_kv, mem_w_o,
           ffn_w_up, ffn_conv_w, ffn_conv_b, ffn_w_down):
    depth = norm_gains.shape[0]
    d_model = x_prompt.shape[-1]
    qk_dim = GDN_QK_HEADS * LANE
    v_dim = GDN_V_HEADS * LANE
    conv_dim = 2 * qk_dim + v_dim
    main_dim = conv_dim + v_dim
    ncols = conv_dim // LANE

    gdn_w_out16, swa_w_out16, mem_w_o16, ffn_w_down16 = (
        w.astype(BF16) for w in (gdn_w_out, swa_w_out, mem_w_o, ffn_w_down))

    def gdn_weights(ia):
        w_tail = gdn_w_in[ia, :, main_dim:]
        w_gate = jnp.zeros((1, d_model, 2 * LANE), F32)
        w_gate = w_gate.at[0, :, :GDN_V_HEADS].set(w_tail[:, :GDN_V_HEADS])
        w_gate = w_gate.at[0, :, LANE:LANE + GDN_V_HEADS].set(w_tail[:, GDN_V_HEADS:])
        cw = gdn_conv_w[ia].reshape(GDN_CONV, ncols, LANE).transpose(1, 0, 2)
        cw = jnp.pad(cw, ((0, 0), (0, SUBLANE - GDN_CONV), (0, 0)))
        gp = jnp.zeros((SUBLANE, LANE), F32)
        gp = gp.at[0, :GDN_V_HEADS].set(gdn_a_log[ia]).at[1, :GDN_V_HEADS].set(gdn_dt_bias[ia])
        return w_gate, cw, gp, gdn_norm[ia]

    def run_trunk(x3, pos, prompt, mem_src):
        batch, seq_len, _ = x3.shape
        m = batch * seq_len
        x = x3.reshape(m, d_model)
        tm1 = _tile_m(m)
        tm2 = _tile_m(m)
        cos_f, sin_f = _rope_tables(pos)
        out = {"gdn_s": [], "gdn_c": [], "swa": [[] for _ in SWA_GROUPS], "ffn": [], "mem_kv": []}
        ia = ib = 0
        for layer in range(depth):
            g = norm_gains[layer]
            if layer % 2 == 0:
                w_gate, cw, gp, o_gain = gdn_weights(ia)
                p = norm_matmul(x, g[0], gdn_w_in, ia, tm=tm1, tn=512, n=main_dim, head_major=True)
                gates = norm_matmul(x, g[0], w_gate, 0, tm=tm1, tn=2 * LANE)
                if prompt:
                    tail0 = jnp.zeros((batch, ncols, SUBLANE, LANE), F32)
                    s0 = jnp.zeros((batch, GDN_V_HEADS, LANE, LANE), F32)
                else:
                    cb = state_gdn_conv[ia].reshape(batch, GDN_CONV - 1, ncols, LANE).transpose(0, 2, 1, 3)
                    tail0 = jnp.pad(cb, ((0, 0), (0, 0), (SUBLANE - (GDN_CONV - 1), 0), (0, 0)))
                    s0 = state_gdn[ia]
                o_hm, s1 = gdn_core(p, gates, cw, gp, o_gain, tail0, s0, batch=batch, seq_len=seq_len)
                out["gdn_s"].append(s1)
                c1 = p.reshape(-1, batch, seq_len, LANE)[:ncols, :, seq_len - (GDN_CONV - 1):]
                out["gdn_c"].append(c1.transpose(1, 2, 0, 3).reshape(batch, GDN_CONV - 1, conv_dim))
                x = matmul_resnorm(o_hm, gdn_w_out16, ia, x, g[1], tm=tm2, tk=512, head_major=True)
                ia += 1
            else:
                ng = len(SWA_GROUPS)
                qkv_hm = norm_matmul(x, g[0], swa_w_qkv, ib, tm=tm1, tn=768, head_major=True)
                if prompt:
                    o_hm, kr = swa_prompt(qkv_hm, cos_f, sin_f, batch=batch, seq_len=seq_len)
                    v_hm = qkv_hm.reshape(3, ng, SWA_HPG, m, LANE)[2]
                    for gi, (window, _) in enumerate(SWA_GROUPS):
                        keep = min(window, seq_len)
                        kv = jnp.stack([_hm_to_tokens(kr[gi], batch, seq_len),
                                        _hm_to_tokens(v_hm[gi], batch, seq_len)], axis=2)
                        out["swa"][gi].append(kv[:, seq_len - keep:])
                else:
                    caches = [c[ib].transpose(0, 1, 3, 2, 4).reshape(batch, -1, seq_len * SWA_HPG * 2, LANE)
                              for c in (cache_swa_kv0, cache_swa_kv1, cache_swa_kv2)]
                    o_hm, rolled = swa_decode(qkv_hm, cos_f, sin_f, caches, batch=batch, steps=seq_len)
                    for gi in range(ng):
                        r5 = rolled[gi].reshape(batch, -1, SWA_HPG, 2, LANE)
                        out["swa"][gi].append(r5.transpose(0, 1, 3, 2, 4))
                x = matmul_resnorm(o_hm, swa_w_out16, ib, x, g[1], tm=tm2, tk=768, head_major=True)
                ib += 1
            q_hm = norm_matmul(x, g[2], mem_w_q, layer, tm=tm1, tn=512, head_major=True)
            if prompt:
                mem_len = mem_src.shape[1]
                kv_hm = norm_matmul(mem_src.reshape(batch * mem_len, d_model), g[2], mem_w_kv, layer,
                                    tm=_tile_m(batch * mem_len), tn=512, do_norm=False, head_major=True)
                out["mem_kv"].append(kv_hm.reshape(2, MEM_HEADS, batch, mem_len, LANE).transpose(2, 3, 0, 1, 4))
                o_hm = mem_prompt(q_hm, kv_hm, batch=batch, seq_len=seq_len, mem_len=mem_len, tq=1024)
            else:
                mem_rows = mem_src.reshape(mem_src.shape[0], batch, -1, LANE)
                o_hm = mem_decode(q_hm, mem_rows, layer, batch=batch, steps=seq_len)
            x = matmul_resnorm(o_hm, mem_w_o16, layer, x, g[3], tm=tm2, tk=512, head_major=True)
            if prompt:
                act, gt = ffn_up(x, g[4], ffn_w_up, layer, ffn_conv_w[layer], ffn_conv_b[layer], seq_len=seq_len,
                                 tm=tm1, tn=512)
                per_seq = seq_len // tm1
                out["ffn"].append(gt[per_seq - 1::per_seq, SUBLANE - (FFN_CONV - 1):])
            else:
                buf = state_ffn_conv[layer]
                zero = jnp.zeros((batch, seq_len - 1, buf.shape[-1]), F32)
                hist1 = jnp.concatenate([buf[:, 1:2], zero], axis=1).reshape(m, -1)
                hist2 = jnp.concatenate([buf, zero[:, 1:]], axis=1).reshape(m, -1)
                act, gt = ffn_up(x, g[4], ffn_w_up, layer, ffn_conv_w[layer], ffn_conv_b[layer], seq_len=seq_len,
                                 tm=tm1, tn=512, hist=(hist1, hist2))
                out["ffn"].append(gt.reshape(batch, seq_len, -1)[:, seq_len - (FFN_CONV - 1):])
            x = matmul_resnorm(act, ffn_w_down16, layer, x, g[5], tm=tm2, tk=512, head_major=False)
        return x.reshape(batch, seq_len, d_model), out

    pos_p = jnp.arange(x_prompt.shape[1])
    y_p, op = run_trunk(x_prompt, pos_p, True, mem_prompt_in)
    pos_s = PAST_LEN + jnp.arange(x_sample.shape[1])
    y_s, os_ = run_trunk(x_sample, pos_s, False, cache_mem_kv)

    return (y_p, y_s, jnp.stack(op["gdn_s"]), jnp.stack(op["gdn_c"]),
            jnp.stack(op["swa"][0]), jnp.stack(op["swa"][1]), jnp.stack(op["swa"][2]),
            jnp.stack(op["mem_kv"]), jnp.stack(op["ffn"]),
            jnp.stack(os_["gdn_s"]), jnp.stack(os_["gdn_c"]),
            jnp.stack(os_["swa"][0]), jnp.stack(os_["swa"][1]), jnp.stack(os_["swa"][2]),
            jnp.stack(os_["ffn"]))
```

```python
import functools
import math

import jax
import jax.numpy as jnp
from jax import lax
from jax.experimental import pallas as pl
from jax.experimental.pallas import tpu as pltpu

F32 = jnp.float32
BF16 = jnp.bfloat16
HIGHEST = lax.Precision.HIGHEST

LANE = 128
SUBLANE = 8
NORM_EPS = 1e-6
ROPE_THETA = 10000.0
PAST_LEN = 2048
GDN_CHUNK = 64
GDN_QK_HEADS = 16
GDN_V_HEADS = 32
GDN_CONV = 4
SWA_GROUPS = ((128, 1), (512, 4), (2048, 16))
SWA_HPG = 6
SWA_SPAN = 128
MEM_HEADS = 4
FFN_CONV = 3
VMEM_LIMIT_BYTES = 52 * 1024 * 1024
MRN_COL_CHUNK = 512


def _params(*semantics):
    return pltpu.CompilerParams(dimension_semantics=semantics, vmem_limit_bytes=VMEM_LIMIT_BYTES)


def _dot(a, b):
    return jnp.dot(a, b, preferred_element_type=F32)


def _dot_nt(a, b):
    return lax.dot_general(a, b, (((1,), (1,)), ((), ())), preferred_element_type=F32)


def _dot_tn(a, b):
    return lax.dot_general(a, b, (((0,), (0,)), ((), ())), preferred_element_type=F32)


def _rms(x, gain):
    return x * lax.rsqrt(jnp.mean(x * x, axis=-1, keepdims=True) + NORM_EPS) * gain


def _silu(x):
    return x * jax.nn.sigmoid(x)


def _nmm_kernel(x_ref, g_ref, w_ref, o_ref, h_ref, *, do_norm, head_major, w_transposed):
    @pl.when(pl.program_id(1) == 0)
    def _():
        x = x_ref[...]
        if do_norm:
            x = _rms(x, g_ref[...])
        h_ref[...] = x.astype(BF16)

    if w_transposed:
        acc = _dot_nt(h_ref[...], w_ref[0].astype(BF16))
    else:
        acc = _dot(h_ref[...], w_ref[0].astype(BF16))
    if head_major:
        for c in range(o_ref.shape[0]):
            o_ref[c] = acc[:, c * LANE:(c + 1) * LANE]
    else:
        o_ref[...] = acc


def norm_matmul(x, gain, w, layer, *, tm, tn, n=None, do_norm=True, head_major=False, w_transposed=False):
    m, k = x.shape
    n = w.shape[1 if w_transposed else 2] if n is None else n
    assert m % tm == 0 and n % tn == 0 and tn % LANE == 0
    if w_transposed:
        w_spec = pl.BlockSpec((1, tn, k), lambda i, j: (layer, j, 0))
    else:
        w_spec = pl.BlockSpec((1, k, tn), lambda i, j: (layer, 0, j))
    if head_major:
        out_shape = jax.ShapeDtypeStruct((n // LANE, m, LANE), F32)
        out_spec = pl.BlockSpec((tn // LANE, tm, LANE), lambda i, j: (j, i, 0))
    else:
        out_shape = jax.ShapeDtypeStruct((m, n), F32)
        out_spec = pl.BlockSpec((tm, tn), lambda i, j: (i, j))
    return pl.pallas_call(
        functools.partial(_nmm_kernel, do_norm=do_norm, head_major=head_major, w_transposed=w_transposed),
        out_shape=out_shape,
        grid=(m // tm, n // tn),
        in_specs=[
            pl.BlockSpec((tm, k), lambda i, j: (i, 0)),
            pl.BlockSpec((1, k), lambda i, j: (0, 0)),
            w_spec,
        ],
        out_specs=out_spec,
        scratch_shapes=[pltpu.VMEM((tm, k), BF16)],
        compiler_params=_params("parallel", "arbitrary"),
        name="norm_matmul",
    )(x, gain.reshape(1, k), w)


def _mrn_kernel(a_ref, w_ref, x_ref, g_ref, o_ref, *, head_major):
    k = pl.program_id(1)
    if head_major:
        a = jnp.concatenate([a_ref[c].astype(BF16) for c in range(a_ref.shape[0])], axis=1)
    else:
        a = a_ref[...].astype(BF16)
    d = o_ref.shape[1]
    cols = [slice(c, c + MRN_COL_CHUNK) for c in range(0, d, MRN_COL_CHUNK)]
    @pl.when(k == 0)
    def _():
        o_ref[...] = jnp.zeros_like(o_ref)

    for cs in cols:
        o_ref[:, cs] += _dot(a, w_ref[0, :, cs])

    @pl.when(k == pl.num_programs(1) - 1)
    def _():
        ss = functools.reduce(lambda x, y: x + y,
                              [jnp.sum(o_ref[:, cs] * o_ref[:, cs], axis=-1, keepdims=True) for cs in cols])
        inv = lax.rsqrt(ss * (1.0 / d) + NORM_EPS)
        for cs in cols:
            o_ref[:, cs] = x_ref[:, cs] + (o_ref[:, cs] * inv) * g_ref[:, cs]


def matmul_resnorm(a, w, layer, x, gain, *, tm, tk, head_major):
    m, d = x.shape
    kdim = w.shape[1]
    assert m % tm == 0 and kdim % tk == 0 and tk % LANE == 0
    if head_major:
        a_spec = pl.BlockSpec((tk // LANE, tm, LANE), lambda i, k: (k, i, 0))
    else:
        a_spec = pl.BlockSpec((tm, tk), lambda i, k: (i, k))
    return pl.pallas_call(
        functools.partial(_mrn_kernel, head_major=head_major),
        out_shape=jax.ShapeDtypeStruct((m, d), F32),
        grid=(m // tm, kdim // tk),
        in_specs=[
            a_spec,
            pl.BlockSpec((1, tk, d), lambda i, k: (layer, k, 0)),
            pl.BlockSpec((tm, d), lambda i, k: (i, 0)),
            pl.BlockSpec((1, d), lambda i, k: (0, 0)),
        ],
        out_specs=pl.BlockSpec((tm, d), lambda i, k: (i, 0)),
        compiler_params=_params("parallel", "arbitrary"),
        name="matmul_resnorm",
    )(a, w, x, gain.reshape(1, d))


def _gelu_tanh(x):
    c = math.sqrt(2.0 / math.pi)
    return x * (0.5 * (1.0 + jnp.tanh(c * (x + 0.044715 * (x * x * x)))))


def _ffn_up_kernel(*refs, seq_len, tm, long_seq):
    if long_seq:
        x_ref, g_ref, wg_ref, wu_ref, cw_ref, cb_ref, act_ref, gt_ref, h_ref, carry_ref = refs
    else:
        x_ref, g_ref, wg_ref, wu_ref, cw_ref, cb_ref, h1_ref, h2_ref, act_ref, gt_ref, h_ref = refs
    i = pl.program_id(0)
    j = pl.program_id(1)

    @pl.when(j == 0)
    def _():
        h_ref[...] = _rms(x_ref[...], g_ref[...]).astype(BF16)

    h = h_ref[...]
    gate = _dot(h, wg_ref[0].astype(BF16))
    up = _dot(h, wu_ref[0].astype(BF16))
    row = lax.broadcasted_iota(jnp.int32, gate.shape, 0)
    sh1 = pltpu.roll(gate, 1, axis=0)
    sh2 = pltpu.roll(gate, 2, axis=0)
    if long_seq:
        prev = jnp.where((i * tm) % seq_len == 0, 0.0, carry_ref[j])
        carry_ref[j] = gate[tm - SUBLANE:tm]
        gt_ref[0] = gate[tm - SUBLANE:tm]
        sh1 = jnp.where(row == 0, prev[7:8], sh1)
        sh2 = jnp.where(row == 0, prev[6:7], jnp.where(row == 1, prev[7:8], sh2))
    else:
        pos = row % SUBLANE
        gt_ref[...] = gate
        sh1 = jnp.where(pos == 0, 0.0, sh1) + h1_ref[...]
        sh2 = jnp.where(pos < 2, 0.0, sh2) + h2_ref[...]
    y = sh2 * cw_ref[0:1] + sh1 * cw_ref[1:2] + gate * cw_ref[2:3]
    act_ref[...] = (_gelu_tanh(y + cb_ref[...]) * up).astype(BF16)


def ffn_up(x, gain, w_up, layer, conv_w, conv_b, *, seq_len, tm, tn, hist=None):
    m, k = x.shape
    f = conv_w.shape[1]
    nj = f // tn
    assert m % tm == 0 and f % tn == 0
    long_seq = seq_len >= tm
    if long_seq:
        assert seq_len % tm == 0
    else:
        assert seq_len == SUBLANE
    in_specs = [
        pl.BlockSpec((tm, k), lambda i, j: (i, 0)),
        pl.BlockSpec((1, k), lambda i, j: (0, 0)),
        pl.BlockSpec((1, k, tn), lambda i, j: (layer, 0, j)),
        pl.BlockSpec((1, k, tn), lambda i, j: (layer, 0, j + nj)),
        pl.BlockSpec((FFN_CONV, tn), lambda i, j: (0, j)),
        pl.BlockSpec((1, tn), lambda i, j: (0, j)),
    ]
    args = [x, gain.reshape(1, k), w_up, w_up, conv_w, conv_b.reshape(1, f)]
    scratch = [pltpu.VMEM((tm, k), BF16)]
    if long_seq:
        gt_shape = jax.ShapeDtypeStruct((m // tm, SUBLANE, f), F32)
        gt_spec = pl.BlockSpec((1, SUBLANE, tn), lambda i, j: (i, 0, j))
        scratch.append(pltpu.VMEM((nj, SUBLANE, tn), F32))
    else:
        gt_shape = jax.ShapeDtypeStruct((m, f), F32)
        gt_spec = pl.BlockSpec((tm, tn), lambda i, j: (i, j))
        in_specs += [pl.BlockSpec((tm, tn), lambda i, j: (i, j))] * 2
        args += list(hist)
    return pl.pallas_call(
        functools.partial(_ffn_up_kernel, seq_len=seq_len, tm=tm, long_seq=long_seq),
        out_shape=(jax.ShapeDtypeStruct((m, f), BF16), gt_shape),
        grid=(m // tm, nj),
        in_specs=in_specs,
        out_specs=(pl.BlockSpec((tm, tn), lambda i, j: (i, j)), gt_spec),
        scratch_shapes=scratch,
        compiler_params=_params("arbitrary", "arbitrary"),
        name="ffn_up",
    )(*args)


def _gdn_kernel(p_ref, g_ref, cw_ref, gp_ref, gain_ref, tail0_ref, s0_ref, o_ref, s_ref,
                tail_ref, gct_ref, *, chunk, heads_per_group):
    c = chunk
    nqk = GDN_QK_HEADS
    rep = GDN_V_HEADS // GDN_QK_HEADS

    @pl.when(pl.program_id(1) == 0)
    def _():
        s_ref[...] = s0_ref[...]
        tail_ref[:, 0:SUBLANE] = tail0_ref[0]

    gates = g_ref[...]
    a_log = gp_ref[0:1, :]
    dt_bias = gp_ref[1:2, :]
    beta_all = jax.nn.sigmoid(gates[:, :LANE])
    xa = gates[:, LANE:] + dt_bias
    softplus = jnp.maximum(xa, 0.0) + jnp.log1p(jnp.exp(-jnp.abs(xa)))
    g_all = -jnp.exp(a_log) * softplus
    row = lax.broadcasted_iota(jnp.int32, (c, c), 0)
    col = lax.broadcasted_iota(jnp.int32, (c, c), 1)
    causal = row >= col
    strict = row > col
    eye = (row == col).astype(F32)
    gc_all = jnp.dot(causal.astype(F32), g_all, precision=HIGHEST, preferred_element_type=F32)
    gct_ref[...] = gc_all.T
    gain = gain_ref[...]

    def conv_silu(cb):
        x = p_ref[cb]
        w = cw_ref[cb]
        tail_ref[cb, SUBLANE:SUBLANE + c] = x
        y = (tail_ref[cb, pl.ds(5, c)] * w[0:1] + tail_ref[cb, pl.ds(6, c)] * w[1:2]
             + tail_ref[cb, pl.ds(7, c)] * w[2:3] + x * w[3:4])
        tail_ref[cb, 0:SUBLANE] = x[c - SUBLANE:c]
        return _silu(y)

    def l2norm(x):
        return x * lax.rsqrt(jnp.sum(x * x, axis=-1, keepdims=True) + NORM_EPS)

    def head_group(qk_heads):
        v_heads = [rep * j + e for j in qk_heads for e in range(rep)]
        q, k, kk, qk = {}, {}, {}, {}
        for j in qk_heads:
            q[j] = l2norm(conv_silu(j)) * (LANE ** -0.5)
            k[j] = l2norm(conv_silu(nqk + j))
        for j in qk_heads:
            k16 = k[j].astype(BF16)
            kk[j] = _dot_nt(k16, k16)
            qk[j] = _dot_nt(q[j].astype(BF16), k16)
        beta_c, gc_c, a_intra, xm, tinv = {}, {}, {}, {}, {}
        for hv in v_heads:
            j = hv // rep
            beta_c[hv] = beta_all[:, hv:hv + 1]
            gc_c[hv] = gc_all[:, hv:hv + 1]
            diff = gc_c[hv] - gct_ref[hv:hv + 1, :]
            decay = jnp.where(causal, jnp.exp(jnp.where(causal, diff, 0.0)), 0.0)
            a_intra[hv] = (qk[j] * decay).astype(BF16)
            xm[hv] = -jnp.where(strict, (beta_c[hv] * kk[j]) * decay, 0.0)
            tinv[hv] = eye + xm[hv]
        for _ in range(int(math.log2(c)) - 1):
            for hv in v_heads:
                xm16 = xm[hv].astype(BF16)
                xm[hv] = _dot(xm16, xm16)
            for hv in v_heads:
                tinv[hv] = tinv[hv] + _dot(tinv[hv].astype(BF16), xm[hv].astype(BF16))
        u, w = {}, {}
        for hv in v_heads:
            j = hv // rep
            tinv16 = tinv[hv].astype(BF16)
            v = conv_silu(2 * nqk + hv)
            u[hv] = _dot(tinv16, (v * beta_c[hv]).astype(BF16))
            w[hv] = _dot(tinv16, (k[j] * (beta_c[hv] * jnp.exp(gc_c[hv]))).astype(BF16))
        v_new16, o_inter = {}, {}
        for hv in v_heads:
            j = hv // rep
            s16 = s_ref[0, hv].astype(BF16)
            v_new16[hv] = (u[hv] - _dot(w[hv].astype(BF16), s16)).astype(BF16)
            o_inter[hv] = _dot((q[j] * jnp.exp(gc_c[hv])).astype(BF16), s16)
        for hv in v_heads:
            j = hv // rep
            gl = gc_c[hv][c - 1:c, :]
            o = o_inter[hv] + _dot(a_intra[hv], v_new16[hv])
            k_dec = k[j] * jnp.exp(gl - gc_c[hv])
            s_ref[0, hv] = s_ref[0, hv] * jnp.exp(gl) + _dot_tn(k_dec.astype(BF16), v_new16[hv])
            z = p_ref[2 * nqk + GDN_V_HEADS + hv]
            o_ref[hv] = (_rms(o, gain) * _silu(z)).astype(o_ref.dtype)

    for j0 in range(0, nqk, heads_per_group):
        head_group(range(j0, j0 + heads_per_group))


def gdn_core(p, gates, conv_w_hm, gate_params, o_gain, tail0, state0, *, batch, seq_len, heads_per_group=GDN_QK_HEADS):
    nblk, m, _ = p.shape
    c = min(GDN_CHUNK, seq_len)
    assert seq_len % c == 0 and c % SUBLANE == 0 and m == batch * seq_len
    nc = seq_len // c
    ncols = 2 * GDN_QK_HEADS + GDN_V_HEADS
    o_dtype = BF16 if c % (2 * SUBLANE) == 0 else F32
    return pl.pallas_call(
        functools.partial(_gdn_kernel, chunk=c, heads_per_group=heads_per_group),
        out_shape=(jax.ShapeDtypeStruct((GDN_V_HEADS, m, LANE), o_dtype),
                   jax.ShapeDtypeStruct((batch, GDN_V_HEADS, LANE, LANE), F32)),
        grid=(batch, nc),
        in_specs=[
            pl.BlockSpec((nblk, c, LANE), lambda b, n: (0, b * nc + n, 0)),
            pl.BlockSpec((c, 2 * LANE), lambda b, n: (b * nc + n, 0)),
            pl.BlockSpec((ncols, SUBLANE, LANE), lambda b, n: (0, 0, 0)),
            pl.BlockSpec((SUBLANE, LANE), lambda b, n: (0, 0)),
            pl.BlockSpec((1, LANE), lambda b, n: (0, 0)),
            pl.BlockSpec((1, ncols, SUBLANE, LANE), lambda b, n: (b, 0, 0, 0)),
            pl.BlockSpec((1, GDN_V_HEADS, LANE, LANE), lambda b, n: (b, 0, 0, 0)),
        ],
        out_specs=(pl.BlockSpec((GDN_V_HEADS, c, LANE), lambda b, n: (0, b * nc + n, 0)),
                   pl.BlockSpec((1, GDN_V_HEADS, LANE, LANE), lambda b, n: (b, 0, 0, 0))),
        scratch_shapes=[pltpu.VMEM((ncols, SUBLANE + c, LANE), F32), pltpu.VMEM((LANE, c), F32)],
        compiler_params=_params("parallel", "arbitrary"),
        name="gdn_core",
    )(p, gates, conv_w_hm, gate_params, o_gain.reshape(1, LANE), tail0, state0)


def _rope(x, cos_f, sin_f):
    return x * cos_f + pltpu.roll(x, LANE // 2, axis=1) * sin_f


def _swa_prompt_kernel(qkv_ref, cos_ref, sin_ref, o_ref, kr_ref, qs_ref, lse_ref, *, seq_len):
    g = pl.program_id(2)
    cos_f = cos_ref[...]
    sin_f = sin_ref[...]
    qs_ref[...] = _rope(qkv_ref[0, 0, 0], cos_f, sin_f)
    kr_ref[0, 0] = _rope(qkv_ref[1, 0, 0], cos_f, sin_f)
    blk = SWA_SPAN
    ii = lax.broadcasted_iota(jnp.int32, (blk, 2 * blk), 0)
    jj = lax.broadcasted_iota(jnp.int32, (blk, 2 * blk), 1)
    dist = ii + blk - jj
    band = (dist >= 0) & (dist <= SWA_SPAN)
    scale = LANE ** -0.5
    units_per_iter = 4
    assert (seq_len // blk) % units_per_iter == 0

    for gi, (window, dil) in enumerate(SWA_GROUPS):
        assert window // dil == SWA_SPAN and seq_len % (blk * dil) == 0
        nblk = seq_len // dil // blk

        @pl.when(g == gi)
        def _(gi=gi, dil=dil, nblk=nblk):
            def units(it, carry):
                cur, first_key, q, kcat, vcat = [], [], [], [], []
                for i in range(units_per_iter):
                    u = it * units_per_iter + i
                    r = u // nblk
                    b = u % nblk
                    cur.append(pl.ds(r + b * blk * dil, blk, stride=dil))
                    prv = pl.ds(r + jnp.maximum(b - 1, 0) * blk * dil, blk, stride=dil)
                    first_key.append(jnp.where(b > 0, 0, blk))
                    q.append(qs_ref[cur[i], :].astype(BF16))
                    kcat.append(jnp.concatenate([kr_ref[0, 0, prv, :], kr_ref[0, 0, cur[i], :]], axis=0).astype(BF16))
                    vcat.append(jnp.concatenate([qkv_ref[2, 0, 0, prv, :], qkv_ref[2, 0, 0, cur[i], :]],
                                                axis=0).astype(BF16))
                s = [_dot_nt(q[i], kcat[i]) * scale for i in range(units_per_iter)]
                p = []
                for i in range(units_per_iter):
                    si = jnp.where(band & (jj >= first_key[i]), s[i], -jnp.inf)
                    mx = jnp.max(si, axis=-1, keepdims=True)
                    e = jnp.exp(si - mx)
                    den = jnp.sum(e, axis=-1, keepdims=True)
                    p.append((e * (1.0 / den)).astype(BF16))
                    lse_ref[gi, cur[i], :] = jnp.broadcast_to(mx + jnp.log(den), (blk, LANE))
                for i in range(units_per_iter):
                    o_ref[gi, 0, cur[i], :] = _dot(p[i], vcat[i])
                return carry

            lax.fori_loop(0, seq_len // blk // units_per_iter, units, 0)

    @pl.when(g == len(SWA_GROUPS) - 1)
    def _():
        lses = [lse_ref[gi] for gi in range(len(SWA_GROUPS))]
        mx = functools.reduce(jnp.maximum, lses)
        es = [jnp.exp(l - mx) for l in lses]
        den = functools.reduce(lambda a, b: a + b, es)
        for gi in range(len(SWA_GROUPS)):
            o_ref[gi, 0] = o_ref[gi, 0] * (es[gi] / den)


def swa_prompt(qkv_hm, cos_f, sin_f, *, batch, seq_len):
    ng = len(SWA_GROUPS)
    m = batch * seq_len
    qkv5 = qkv_hm.reshape(3, ng, SWA_HPG, m, LANE)
    o, kr = pl.pallas_call(
        functools.partial(_swa_prompt_kernel, seq_len=seq_len),
        out_shape=(jax.ShapeDtypeStruct((ng, SWA_HPG, m, LANE), F32),
                   jax.ShapeDtypeStruct((ng, SWA_HPG, m, LANE), F32)),
        grid=(batch, SWA_HPG, ng),
        in_specs=[
            pl.BlockSpec((3, 1, 1, seq_len, LANE), lambda b, h, g: (0, g, h, b, 0)),
            pl.BlockSpec((seq_len, LANE), lambda b, h, g: (0, 0)),
            pl.BlockSpec((seq_len, LANE), lambda b, h, g: (0, 0)),
        ],
        out_specs=(pl.BlockSpec((ng, 1, seq_len, LANE), lambda b, h, g: (0, h, b, 0)),
                   pl.BlockSpec((1, 1, seq_len, LANE), lambda b, h, g: (g, h, b, 0))),
        scratch_shapes=[pltpu.VMEM((seq_len, LANE), F32), pltpu.VMEM((ng, seq_len, LANE), F32)],
        compiler_params=_params("parallel", "parallel", "arbitrary"),
        name="swa_prompt",
    )(qkv5, cos_f, sin_f)
    return o.reshape(ng * SWA_HPG, m, LANE), kr


def _kv_pack_kernel(k_ref, v_ref, o_ref):
    nh, rows, _ = k_ref.shape[1:]
    for h in range(nh):
        o_ref[0, pl.ds(2 * h, rows, stride=2 * nh), :] = k_ref[0, h]
        o_ref[0, pl.ds(2 * h + 1, rows, stride=2 * nh), :] = v_ref[h]


def kv_pack(k_rot, qkv_hm, group, *, batch, seq_len, keep):
    ng, nh = k_rot.shape[:2]
    rows = min(keep, 512)
    assert keep % rows == 0 and (seq_len - keep) % rows == 0
    first = (seq_len - keep) // rows
    per_seq = seq_len // rows
    return pl.pallas_call(
        _kv_pack_kernel,
        out_shape=jax.ShapeDtypeStruct((batch, keep * nh * 2, LANE), F32),
        grid=(batch, keep // rows),
        in_specs=[
            pl.BlockSpec((1, nh, rows, LANE), lambda b, c: (group, 0, b * per_seq + first + c, 0)),
            pl.BlockSpec((nh, rows, LANE), lambda b, c: (2 * ng + group, b * per_seq + first + c, 0)),
        ],
        out_specs=pl.BlockSpec((1, rows * nh * 2, LANE), lambda b, c: (b, c, 0)),
        compiler_params=_params("parallel", "parallel"),
        name="kv_pack",
    )(k_rot, qkv_hm)


def _mem_prompt_kernel(q_ref, k_ref, v_ref, o_ref):
    s = _dot_nt(q_ref[0].astype(BF16), k_ref[0].astype(BF16)) * (LANE ** -0.5)
    mx = jnp.max(s, axis=-1, keepdims=True)
    e = jnp.exp(s - mx)
    p = e / jnp.sum(e, axis=-1, keepdims=True)
    o_ref[0] = _dot(p.astype(BF16), v_ref[0].astype(BF16))


def mem_prompt(q_hm, kv_hm, *, batch, seq_len, mem_len, tq):
    m = batch * seq_len
    nq = seq_len // tq
    return pl.pallas_call(
        _mem_prompt_kernel,
        out_shape=jax.ShapeDtypeStruct((MEM_HEADS, m, LANE), F32),
        grid=(batch, MEM_HEADS, nq),
        in_specs=[
            pl.BlockSpec((1, tq, LANE), lambda b, h, i: (h, b * nq + i, 0)),
            pl.BlockSpec((1, mem_len, LANE), lambda b, h, i: (h, b, 0)),
            pl.BlockSpec((1, mem_len, LANE), lambda b, h, i: (MEM_HEADS + h, b, 0)),
        ],
        out_specs=pl.BlockSpec((1, tq, LANE), lambda b, h, i: (h, b * nq + i, 0)),
        compiler_params=_params("parallel", "parallel", "parallel"),
        name="mem_prompt",
    )(q_hm, kv_hm, kv_hm)


def _mem_decode_kernel(q_ref, c_ref, o_ref, *, steps):
    nh = q_ref.shape[0]
    nb = c_ref.shape[1]
    mem_len = c_ref.shape[2] // (2 * nh)
    pairs = [(i, h) for i in range(nb) for h in range(nh)]
    s, v = {}, {}
    for i, h in pairs:
        k = c_ref[0, i, pl.ds(h, mem_len, stride=2 * nh), :].astype(BF16)
        v[i, h] = c_ref[0, i, pl.ds(nh + h, mem_len, stride=2 * nh), :].astype(BF16)
        q = q_ref[h, i * steps:(i + 1) * steps, :].astype(BF16)
        s[i, h] = _dot_nt(q, k) * (LANE ** -0.5)
    p = {}
    for key in pairs:
        e = jnp.exp(s[key] - jnp.max(s[key], axis=-1, keepdims=True))
        p[key] = (e / jnp.sum(e, axis=-1, keepdims=True)).astype(BF16)
    for i, h in pairs:
        o_ref[h, i * steps:(i + 1) * steps, :] = _dot(p[i, h], v[i, h])


def mem_decode(q_hm, cache, layer, *, batch, steps, batch_per_step=4):
    h = q_hm.shape[0]
    nb = batch_per_step if batch % batch_per_step == 0 else 1
    return pl.pallas_call(
        functools.partial(_mem_decode_kernel, steps=steps),
        out_shape=jax.ShapeDtypeStruct(q_hm.shape, F32),
        grid=(batch // nb,),
        in_specs=[
            pl.BlockSpec((h, nb * steps, LANE), lambda b: (0, b, 0)),
            pl.BlockSpec((1, nb, cache.shape[2], LANE), lambda b: (layer, b, 0, 0)),
        ],
        out_specs=pl.BlockSpec((h, nb * steps, LANE), lambda b: (0, b, 0)),
        compiler_params=_params("parallel"),
        name="mem_decode",
    )(q_hm, cache)


def _swa_decode_kernel(qkv_ref, cos_ref, sin_ref, c0_ref, c1_ref, c2_ref,
                       o_ref, r0_ref, r1_ref, r2_ref, n0_ref, n1_ref, n2_ref, sem, *, steps):
    b = pl.program_id(0)
    ng = len(SWA_GROUPS)
    caches = (c0_ref, c1_ref, c2_ref)
    rolled = (r0_ref, r1_ref, r2_ref)
    staged = (n0_ref, n1_ref, n2_ref)
    rows_per_t = 2 * SWA_HPG
    scale = LANE ** -0.5

    def roll_copy(g):
        keep = caches[g].shape[1] - 1
        return pltpu.make_async_copy(caches[g].at[0, pl.ds(1, keep)], rolled[g].at[b, pl.ds(0, keep)], sem.at[g, 0])

    def new_copy(g):
        return pltpu.make_async_copy(staged[g], rolled[g].at[b, caches[g].shape[1] - 1], sem.at[g, 1])

    for g in range(ng):
        roll_copy(g).start()

    cos_f = cos_ref[...]
    sin_f = sin_ref[...]
    u_col = lax.broadcasted_iota(jnp.int32, (steps, 1), 0)
    outs, lses = {}, {}
    for g, (window, dil) in enumerate(SWA_GROUPS):
        chunk_stride = max(dil // steps, 1)
        n_chunks = caches[g].shape[1] // chunk_stride
        shifts = list(range(0, steps, dil)) if dil < steps else [0]
        c_idx = lax.broadcasted_iota(jnp.int32, (n_chunks, steps, 1), 0)
        u_idx = lax.broadcasted_iota(jnp.int32, (n_chunks, steps, 1), 1)
        tau = c_idx * (steps * chunk_stride) + u_idx
        for h in range(SWA_HPG):
            hd = g * SWA_HPG + h
            q = _rope(qkv_ref[hd], cos_f, sin_f)
            k_new = _rope(qkv_ref[ng * SWA_HPG + hd], cos_f, sin_f)
            v_new = qkv_ref[2 * ng * SWA_HPG + hd]
            staged[g][pl.ds(2 * h, steps, stride=rows_per_t), :] = k_new
            staged[g][pl.ds(2 * h + 1, steps, stride=rows_per_t), :] = v_new
            k_c = caches[g][0, pl.ds(0, n_chunks, stride=chunk_stride), pl.ds(2 * h, steps, stride=rows_per_t), :]
            v_c = caches[g][0, pl.ds(0, n_chunks, stride=chunk_stride), pl.ds(2 * h + 1, steps, stride=rows_per_t), :]
            parts = []
            for a in shifts:
                q_a = q if a == 0 else pltpu.roll(q, steps - a, axis=0)
                sc = jnp.sum(k_c * q_a[None], axis=-1, keepdims=True) * scale
                sn = jnp.sum(k_new * q_a, axis=-1, keepdims=True) * scale
                if a > 0:
                    s_idx = (u_idx + a) & (steps - 1)
                    sc = jnp.where(tau >= s_idx, sc, -jnp.inf)
                    sn = jnp.where(u_col + a < steps, sn, -jnp.inf)
                mx = jnp.maximum(jnp.max(sc, axis=0), sn)
                e = jnp.exp(sc - mx[None])
                en = jnp.exp(sn - mx)
                den = jnp.sum(e, axis=0) + en
                acc = jnp.sum(e * v_c, axis=0) + en * v_new
                mx = jnp.broadcast_to(mx, (steps, LANE))
                den = jnp.broadcast_to(den, (steps, LANE))
                if a > 0:
                    mx, den, acc = (pltpu.roll(x, a, axis=0) for x in (mx, den, acc))
                parts.append((mx, den, acc))
            mx = functools.reduce(jnp.maximum, [p[0] for p in parts])
            ws = [jnp.exp(p[0] - mx) for p in parts]
            den = functools.reduce(lambda x, y: x + y, [w * p[1] for w, p in zip(ws, parts)])
            acc = functools.reduce(lambda x, y: x + y, [w * p[2] for w, p in zip(ws, parts)])
            outs[g, h] = acc / den
            lses[g, h] = mx + jnp.log(den)

    for g in range(ng):
        new_copy(g).start()

    for h in range(SWA_HPG):
        mx = functools.reduce(jnp.maximum, [lses[g, h] for g in range(ng)])
        es = [jnp.exp(lses[g, h] - mx) for g in range(ng)]
        den = functools.reduce(lambda x, y: x + y, es)
        for g in range(ng):
            o_ref[g * SWA_HPG + h] = outs[g, h] * (es[g] / den)

    for g in range(ng):
        roll_copy(g).wait()
        new_copy(g).wait()


def swa_decode(qkv_hm, cos_f, sin_f, caches, *, batch, steps):
    ng = len(SWA_GROUPS)
    assert steps & (steps - 1) == 0 and steps % SUBLANE == 0
    specs = []
    for (window, dil), c in zip(SWA_GROUPS, caches):
        assert c.shape[1] * steps == window and window // dil == SWA_SPAN, "cache must hold exactly one window"
        assert dil % steps == 0 or steps % dil == 0
        specs.append(pl.BlockSpec((1,) + c.shape[1:], lambda b: (b, 0, 0, 0)))
    any_spec = pl.BlockSpec(memory_space=pl.ANY)
    out = pl.pallas_call(
        functools.partial(_swa_decode_kernel, steps=steps),
        out_shape=(jax.ShapeDtypeStruct((ng * SWA_HPG, batch * steps, LANE), F32),)
        + tuple(jax.ShapeDtypeStruct(c.shape, F32) for c in caches),
        grid=(batch,),
        in_specs=[
            pl.BlockSpec((3 * ng * SWA_HPG, steps, LANE), lambda b: (0, b, 0)),
            pl.BlockSpec((steps, LANE), lambda b: (0, 0)),
            pl.BlockSpec((steps, LANE), lambda b: (0, 0)),
        ] + specs,
        out_specs=(pl.BlockSpec((ng * SWA_HPG, steps, LANE), lambda b: (0, b, 0)),) + (any_spec,) * ng,
        scratch_shapes=[pltpu.VMEM((steps * SWA_HPG * 2, LANE), F32)] * ng + [pltpu.SemaphoreType.DMA((ng, 2))],
        compiler_params=_params("arbitrary"),
        name="swa_decode",
    )(qkv_hm, cos_f, sin_f, *caches)
    return out[0], out[1:]


def _rope_tables(pos):
    half = LANE // 2
    inv = ROPE_THETA ** (-jnp.arange(half, dtype=F32) / half)
    ang = pos.astype(F32)[:, None] * inv[None, :]
    cos, sin = jnp.cos(ang), jnp.sin(ang)
    return jnp.concatenate([cos, cos], axis=1), jnp.concatenate([-sin, sin], axis=1)


def _tile_m(m):
    return 1024 if m % 1024 == 0 else m


def kernel(x_prompt, x_sample, state_gdn, state_gdn_conv, cache_swa_kv0, cache_swa_kv1, cache_swa_kv2,
           cache_mem_kv, state_ffn_conv, mem_prompt_in, norm_gains, gdn_w_in, gdn_conv_w, gdn_a_log,
           gdn_dt_bias, gdn_norm, gdn_w_out, swa_w_qkv, swa_w_out, mem_w_q, mem_w_kv, mem_w_o,
           ffn_w_up, ffn_conv_w, ffn_conv_b, ffn_w_down):
    depth = norm_gains.shape[0]
    d_model = x_prompt.shape[-1]
    qk_dim = GDN_QK_HEADS * LANE
    v_dim = GDN_V_HEADS * LANE
    conv_dim = 2 * qk_dim + v_dim
    main_dim = conv_dim + v_dim
    ncols = conv_dim // LANE

    gdn_w_out16, swa_w_out16, mem_w_o16, ffn_w_down16 = (
        w.astype(BF16) for w in (gdn_w_out, swa_w_out, mem_w_o, ffn_w_down))
    gdn_w_in_t = jnp.swapaxes(gdn_w_in, 1, 2)

    def gdn_weights(ia):
        w_tail = gdn_w_in[ia, :, main_dim:]
        w_gate = jnp.zeros((1, d_model, 2 * LANE), F32)
        w_gate = w_gate.at[0, :, :GDN_V_HEADS].set(w_tail[:, :GDN_V_HEADS])
        w_gate = w_gate.at[0, :, LANE:LANE + GDN_V_HEADS].set(w_tail[:, GDN_V_HEADS:])
        cw = gdn_conv_w[ia].reshape(GDN_CONV, ncols, LANE).transpose(1, 0, 2)
        cw = jnp.pad(cw, ((0, 0), (0, SUBLANE - GDN_CONV), (0, 0)))
        gp = jnp.zeros((SUBLANE, LANE), F32)
        gp = gp.at[0, :GDN_V_HEADS].set(gdn_a_log[ia]).at[1, :GDN_V_HEADS].set(gdn_dt_bias[ia])
        return w_gate, cw, gp, gdn_norm[ia]

    def run_trunk(x3, pos, prompt, mem_src):
        batch, seq_len, _ = x3.shape
        m = batch * seq_len
        x = x3.reshape(m, d_model)
        tm1 = _tile_m(m)
        tm2 = _tile_m(m)
        cos_f, sin_f = _rope_tables(pos)
        out = {"gdn_s": [], "gdn_c": [], "swa": [[] for _ in SWA_GROUPS], "ffn": [], "mem_kv": []}
        ia = ib = 0
        for layer in range(depth):
            g = norm_gains[layer]
            if layer % 2 == 0:
                w_gate, cw, gp, o_gain = gdn_weights(ia)
                p = norm_matmul(x, g[0], gdn_w_in_t, ia, tm=tm1, tn=512, n=main_dim, head_major=True,
                                w_transposed=True)
                gates = norm_matmul(x, g[0], w_gate, 0, tm=tm1, tn=2 * LANE)
                if prompt:
                    tail0 = jnp.zeros((batch, ncols, SUBLANE, LANE), F32)
                    s0 = jnp.zeros((batch, GDN_V_HEADS, LANE, LANE), F32)
                else:
                    cb = state_gdn_conv[ia].reshape(batch, GDN_CONV - 1, ncols, LANE).transpose(0, 2, 1, 3)
                    tail0 = jnp.pad(cb, ((0, 0), (0, 0), (SUBLANE - (GDN_CONV - 1), 0), (0, 0)))
                    s0 = state_gdn[ia]
                o_hm, s1 = gdn_core(p, gates, cw, gp, o_gain, tail0, s0, batch=batch, seq_len=seq_len)
                out["gdn_s"].append(s1)
                c1 = p.reshape(-1, batch, seq_len, LANE)[:ncols, :, seq_len - (GDN_CONV - 1):]
                out["gdn_c"].append(c1.transpose(1, 2, 0, 3).reshape(batch, GDN_CONV - 1, conv_dim))
                x = matmul_resnorm(o_hm, gdn_w_out16, ia, x, g[1], tm=tm2, tk=512, head_major=True)
                ia += 1
            else:
                ng = len(SWA_GROUPS)
                qkv_hm = norm_matmul(x, g[0], swa_w_qkv, ib, tm=tm1, tn=768, head_major=True)
                if prompt:
                    o_hm, kr = swa_prompt(qkv_hm, cos_f, sin_f, batch=batch, seq_len=seq_len)
                    for gi, (window, _) in enumerate(SWA_GROUPS):
                        keep = min(window, seq_len)
                        kv = kv_pack(kr, qkv_hm, gi, batch=batch, seq_len=seq_len, keep=keep)
                        out["swa"][gi].append(kv.reshape(batch, keep, SWA_HPG, 2, LANE).transpose(0, 1, 3, 2, 4))
                else:
                    caches = [c[ib].transpose(0, 1, 3, 2, 4).reshape(batch, -1, seq_len * SWA_HPG * 2, LANE)
                              for c in (cache_swa_kv0, cache_swa_kv1, cache_swa_kv2)]
                    o_hm, rolled = swa_decode(qkv_hm, cos_f, sin_f, caches, batch=batch, steps=seq_len)
                    for gi in range(ng):
                        r5 = rolled[gi].reshape(batch, -1, SWA_HPG, 2, LANE)
                        out["swa"][gi].append(r5.transpose(0, 1, 3, 2, 4))
                x = matmul_resnorm(o_hm, swa_w_out16, ib, x, g[1], tm=tm2, tk=768, head_major=True)
                ib += 1
            q_hm = norm_matmul(x, g[2], mem_w_q, layer, tm=tm1, tn=512, head_major=True)
            if prompt:
                mem_len = mem_src.shape[1]
                kv_hm = norm_matmul(mem_src.reshape(batch * mem_len, d_model), g[2], mem_w_kv, layer,
                                    tm=_tile_m(batch * mem_len), tn=512, do_norm=False, head_major=True)
                out["mem_kv"].append(kv_hm.reshape(2, MEM_HEADS, batch, mem_len, LANE).transpose(2, 3, 0, 1, 4))
                o_hm = mem_prompt(q_hm, kv_hm, batch=batch, seq_len=seq_len, mem_len=mem_len, tq=1024)
            else:
                mem_rows = mem_src.reshape(mem_src.shape[0], batch, -1, LANE)
                o_hm = mem_decode(q_hm, mem_rows, layer, batch=batch, steps=seq_len)
            x = matmul_resnorm(o_hm, mem_w_o16, layer, x, g[3], tm=tm2, tk=512, head_major=True)
            if prompt:
                act, gt = ffn_up(x, g[4], ffn_w_up, layer, ffn_conv_w[layer], ffn_conv_b[layer], seq_len=seq_len,
                                 tm=tm1, tn=512)
                per_seq = seq_len // tm1
                out["ffn"].append(gt[per_seq - 1::per_seq, SUBLANE - (FFN_CONV - 1):])
            else:
                buf = state_ffn_conv[layer]
                zero = jnp.zeros((batch, seq_len - 1, buf.shape[-1]), F32)
                hist1 = jnp.concatenate([buf[:, 1:2], zero], axis=1).reshape(m, -1)
                hist2 = jnp.concatenate([buf, zero[:, 1:]], axis=1).reshape(m, -1)
                act, gt = ffn_up(x, g[4], ffn_w_up, layer, ffn_conv_w[layer], ffn_conv_b[layer], seq_len=seq_len,
                                 tm=tm1, tn=512, hist=(hist1, hist2))
                out["ffn"].append(gt.reshape(batch, seq_len, -1)[:, seq_len - (FFN_CONV - 1):])
            x = matmul_resnorm(act, ffn_w_down16, layer, x, g[5], tm=tm2, tk=512, head_major=False)
        return x.reshape(batch, seq_len, d_model), out

    pos_p = jnp.arange(x_prompt.shape[1])
    y_p, op = run_trunk(x_prompt, pos_p, True, mem_prompt_in)
    pos_s = PAST_LEN + jnp.arange(x_sample.shape[1])
    y_s, os_ = run_trunk(x_sample, pos_s, False, cache_mem_kv)

    return (y_p, y_s, jnp.stack(op["gdn_s"]), jnp.stack(op["gdn_c"]),
            jnp.stack(op["swa"][0]), jnp.stack(op["swa"][1]), jnp.stack(op["swa"][2]),
            jnp.stack(op["mem_kv"]), jnp.stack(op["ffn"]),
            jnp.stack(os_["gdn_s"]), jnp.stack(os_["gdn_c"]),
            jnp.stack(os_["swa"][0]), jnp.stack(os_["swa"][1]), jnp.stack(os_["swa"][2]),
            jnp.stack(os_["ffn"]))
```

```python
import functools
import math

import jax
import jax.numpy as jnp
from jax import lax
from jax.experimental import pallas as pl
from jax.experimental.pallas import tpu as pltpu

F32 = jnp.float32
BF16 = jnp.bfloat16
HIGHEST = lax.Precision.HIGHEST

LANE = 128
SUBLANE = 8
NORM_EPS = 1e-6
ROPE_THETA = 10000.0
PAST_LEN = 2048
GDN_CHUNK = 64
GDN_QK_HEADS = 16
GDN_V_HEADS = 32
GDN_CONV = 4
SWA_GROUPS = ((128, 1), (512, 4), (2048, 16))
SWA_HPG = 6
SWA_SPAN = 128
MEM_HEADS = 4
FFN_CONV = 3
VMEM_LIMIT_BYTES = 52 * 1024 * 1024
MRN_COL_CHUNK = 512


def _params(*semantics):
    return pltpu.CompilerParams(dimension_semantics=semantics, vmem_limit_bytes=VMEM_LIMIT_BYTES)


def _dot(a, b):
    return jnp.dot(a, b, preferred_element_type=F32)


def _dot_nt(a, b):
    return lax.dot_general(a, b, (((1,), (1,)), ((), ())), preferred_element_type=F32)


def _dot_tn(a, b):
    return lax.dot_general(a, b, (((0,), (0,)), ((), ())), preferred_element_type=F32)


def _rms(x, gain):
    return x * lax.rsqrt(jnp.mean(x * x, axis=-1, keepdims=True) + NORM_EPS) * gain


def _silu(x):
    return x * jax.nn.sigmoid(x)


def _nmm_kernel(x_ref, g_ref, w_ref, o_ref, h_ref, *, do_norm, head_major, w_transposed):
    @pl.when(pl.program_id(1) == 0)
    def _():
        x = x_ref[...]
        if do_norm:
            x = _rms(x, g_ref[...])
        h_ref[...] = x.astype(BF16)

    if w_transposed:
        acc = _dot_nt(h_ref[...], w_ref[0].astype(BF16))
    else:
        acc = _dot(h_ref[...], w_ref[0].astype(BF16))
    if head_major:
        for c in range(o_ref.shape[0]):
            o_ref[c] = acc[:, c * LANE:(c + 1) * LANE]
    else:
        o_ref[...] = acc


def norm_matmul(x, gain, w, layer, *, tm, tn, n=None, do_norm=True, head_major=False, w_transposed=False):
    m, k = x.shape
    n = w.shape[1 if w_transposed else 2] if n is None else n
    assert m % tm == 0 and n % tn == 0 and tn % LANE == 0
    if w_transposed:
        w_spec = pl.BlockSpec((1, tn, k), lambda i, j: (layer, j, 0))
    else:
        w_spec = pl.BlockSpec((1, k, tn), lambda i, j: (layer, 0, j))
    if head_major:
        out_shape = jax.ShapeDtypeStruct((n // LANE, m, LANE), F32)
        out_spec = pl.BlockSpec((tn // LANE, tm, LANE), lambda i, j: (j, i, 0))
    else:
        out_shape = jax.ShapeDtypeStruct((m, n), F32)
        out_spec = pl.BlockSpec((tm, tn), lambda i, j: (i, j))
    return pl.pallas_call(
        functools.partial(_nmm_kernel, do_norm=do_norm, head_major=head_major, w_transposed=w_transposed),
        out_shape=out_shape,
        grid=(m // tm, n // tn),
        in_specs=[
            pl.BlockSpec((tm, k), lambda i, j: (i, 0)),
            pl.BlockSpec((1, k), lambda i, j: (0, 0)),
            w_spec,
        ],
        out_specs=out_spec,
        scratch_shapes=[pltpu.VMEM((tm, k), BF16)],
        compiler_params=_params("parallel", "arbitrary"),
        name="norm_matmul",
    )(x, gain.reshape(1, k), w)


def _mrn_kernel(a_ref, w_ref, x_ref, g_ref, o_ref, *, head_major):
    k = pl.program_id(1)
    if head_major:
        a = jnp.concatenate([a_ref[c].astype(BF16) for c in range(a_ref.shape[0])], axis=1)
    else:
        a = a_ref[...].astype(BF16)
    d = o_ref.shape[1]
    cols = [slice(c, c + MRN_COL_CHUNK) for c in range(0, d, MRN_COL_CHUNK)]
    @pl.when(k == 0)
    def _():
        o_ref[...] = jnp.zeros_like(o_ref)

    for cs in cols:
        o_ref[:, cs] += _dot(a, w_ref[0, :, cs])

    @pl.when(k == pl.num_programs(1) - 1)
    def _():
        ss = functools.reduce(lambda x, y: x + y,
                              [jnp.sum(o_ref[:, cs] * o_ref[:, cs], axis=-1, keepdims=True) for cs in cols])
        inv = lax.rsqrt(ss * (1.0 / d) + NORM_EPS)
        for cs in cols:
            o_ref[:, cs] = x_ref[:, cs] + (o_ref[:, cs] * inv) * g_ref[:, cs]


def matmul_resnorm(a, w, layer, x, gain, *, tm, tk, head_major):
    m, d = x.shape
    kdim = w.shape[1]
    assert m % tm == 0 and kdim % tk == 0 and tk % LANE == 0
    if head_major:
        a_spec = pl.BlockSpec((tk // LANE, tm, LANE), lambda i, k: (k, i, 0))
    else:
        a_spec = pl.BlockSpec((tm, tk), lambda i, k: (i, k))
    return pl.pallas_call(
        functools.partial(_mrn_kernel, head_major=head_major),
        out_shape=jax.ShapeDtypeStruct((m, d), F32),
        grid=(m // tm, kdim // tk),
        in_specs=[
            a_spec,
            pl.BlockSpec((1, tk, d), lambda i, k: (layer, k, 0)),
            pl.BlockSpec((tm, d), lambda i, k: (i, 0)),
            pl.BlockSpec((1, d), lambda i, k: (0, 0)),
        ],
        out_specs=pl.BlockSpec((tm, d), lambda i, k: (i, 0)),
        compiler_params=_params("parallel", "arbitrary"),
        name="matmul_resnorm",
    )(a, w, x, gain.reshape(1, d))


def _gelu_tanh(x):
    c = math.sqrt(2.0 / math.pi)
    return x * (0.5 * (1.0 + jnp.tanh(c * (x + 0.044715 * (x * x * x)))))


def _ffn_up_kernel(*refs, seq_len, tm, long_seq):
    if long_seq:
        x_ref, g_ref, wg_ref, wu_ref, cw_ref, cb_ref, act_ref, gt_ref, h_ref, carry_ref = refs
    else:
        x_ref, g_ref, wg_ref, wu_ref, cw_ref, cb_ref, h1_ref, h2_ref, act_ref, gt_ref, h_ref = refs
    i = pl.program_id(0)
    j = pl.program_id(1)

    @pl.when(j == 0)
    def _():
        h_ref[...] = _rms(x_ref[...], g_ref[...]).astype(BF16)

    h = h_ref[...]
    gate = _dot(h, wg_ref[0].astype(BF16))
    up = _dot(h, wu_ref[0].astype(BF16))
    row = lax.broadcasted_iota(jnp.int32, gate.shape, 0)
    sh1 = pltpu.roll(gate, 1, axis=0)
    sh2 = pltpu.roll(gate, 2, axis=0)
    if long_seq:
        prev = jnp.where((i * tm) % seq_len == 0, 0.0, carry_ref[j])
        carry_ref[j] = gate[tm - SUBLANE:tm]
        gt_ref[0] = gate[tm - SUBLANE:tm]
        sh1 = jnp.where(row == 0, prev[7:8], sh1)
        sh2 = jnp.where(row == 0, prev[6:7], jnp.where(row == 1, prev[7:8], sh2))
    else:
        pos = row % SUBLANE
        gt_ref[...] = gate
        sh1 = jnp.where(pos == 0, 0.0, sh1) + h1_ref[...]
        sh2 = jnp.where(pos < 2, 0.0, sh2) + h2_ref[...]
    y = sh2 * cw_ref[0:1] + sh1 * cw_ref[1:2] + gate * cw_ref[2:3]
    act_ref[...] = (_gelu_tanh(y + cb_ref[...]) * up).astype(BF16)


def ffn_up(x, gain, w_up, layer, conv_w, conv_b, *, seq_len, tm, tn, hist=None):
    m, k = x.shape
    f = conv_w.shape[1]
    nj = f // tn
    assert m % tm == 0 and f % tn == 0
    long_seq = seq_len >= tm
    if long_seq:
        assert seq_len % tm == 0
    else:
        assert seq_len == SUBLANE
    in_specs = [
        pl.BlockSpec((tm, k), lambda i, j: (i, 0)),
        pl.BlockSpec((1, k), lambda i, j: (0, 0)),
        pl.BlockSpec((1, k, tn), lambda i, j: (layer, 0, j)),
        pl.BlockSpec((1, k, tn), lambda i, j: (layer, 0, j + nj)),
        pl.BlockSpec((FFN_CONV, tn), lambda i, j: (0, j)),
        pl.BlockSpec((1, tn), lambda i, j: (0, j)),
    ]
    args = [x, gain.reshape(1, k), w_up, w_up, conv_w, conv_b.reshape(1, f)]
    scratch = [pltpu.VMEM((tm, k), BF16)]
    if long_seq:
        gt_shape = jax.ShapeDtypeStruct((m // tm, SUBLANE, f), F32)
        gt_spec = pl.BlockSpec((1, SUBLANE, tn), lambda i, j: (i, 0, j))
        scratch.append(pltpu.VMEM((nj, SUBLANE, tn), F32))
    else:
        gt_shape = jax.ShapeDtypeStruct((m, f), F32)
        gt_spec = pl.BlockSpec((tm, tn), lambda i, j: (i, j))
        in_specs += [pl.BlockSpec((tm, tn), lambda i, j: (i, j))] * 2
        args += list(hist)
    return pl.pallas_call(
        functools.partial(_ffn_up_kernel, seq_len=seq_len, tm=tm, long_seq=long_seq),
        out_shape=(jax.ShapeDtypeStruct((m, f), BF16), gt_shape),
        grid=(m // tm, nj),
        in_specs=in_specs,
        out_specs=(pl.BlockSpec((tm, tn), lambda i, j: (i, j)), gt_spec),
        scratch_shapes=scratch,
        compiler_params=_params("arbitrary", "arbitrary"),
        name="ffn_up",
    )(*args)


def _gdn_kernel(p_ref, g_ref, cw_ref, gp_ref, gain_ref, tail0_ref, s0_ref, o_ref, s_ref,
                tail_ref, gct_ref, *, chunk, heads_per_group):
    c = chunk
    nqk = GDN_QK_HEADS
    rep = GDN_V_HEADS // GDN_QK_HEADS
    stack_state_rows = 2 * c <= 2 * SUBLANE

    @pl.when(pl.program_id(1) == 0)
    def _():
        s_ref[...] = s0_ref[...]
        tail_ref[:, 0:SUBLANE] = tail0_ref[0]

    gates = g_ref[...]
    a_log = gp_ref[0:1, :]
    dt_bias = gp_ref[1:2, :]
    beta_all = jax.nn.sigmoid(gates[:, :LANE])
    xa = gates[:, LANE:] + dt_bias
    softplus = jnp.maximum(xa, 0.0) + jnp.log1p(jnp.exp(-jnp.abs(xa)))
    g_all = -jnp.exp(a_log) * softplus
    row = lax.broadcasted_iota(jnp.int32, (c, c), 0)
    col = lax.broadcasted_iota(jnp.int32, (c, c), 1)
    causal = row >= col
    strict = row > col
    eye = (row == col).astype(F32)
    gc_all = jnp.dot(causal.astype(F32), g_all, precision=HIGHEST, preferred_element_type=F32)
    gct_ref[...] = gc_all.T
    gain = gain_ref[...]

    def conv_silu(cb):
        x = p_ref[cb]
        w = cw_ref[cb]
        tail_ref[cb, SUBLANE:SUBLANE + c] = x
        y = (tail_ref[cb, pl.ds(5, c)] * w[0:1] + tail_ref[cb, pl.ds(6, c)] * w[1:2]
             + tail_ref[cb, pl.ds(7, c)] * w[2:3] + x * w[3:4])
        tail_ref[cb, 0:SUBLANE] = x[c - SUBLANE:c]
        return _silu(y)

    def l2norm(x):
        return x * lax.rsqrt(jnp.sum(x * x, axis=-1, keepdims=True) + NORM_EPS)

    def head_group(qk_heads):
        v_heads = [rep * j + e for j in qk_heads for e in range(rep)]
        q, k, kk, qk = {}, {}, {}, {}
        for j in qk_heads:
            q[j] = l2norm(conv_silu(j)) * (LANE ** -0.5)
            k[j] = l2norm(conv_silu(nqk + j))
        for j in qk_heads:
            k16 = k[j].astype(BF16)
            kk[j] = _dot_nt(k16, k16)
            qk[j] = _dot_nt(q[j].astype(BF16), k16)
        beta_c, gc_c, a_intra, xm, tinv = {}, {}, {}, {}, {}
        for hv in v_heads:
            j = hv // rep
            beta_c[hv] = beta_all[:, hv:hv + 1]
            gc_c[hv] = gc_all[:, hv:hv + 1]
            diff = gc_c[hv] - gct_ref[hv:hv + 1, :]
            decay = jnp.where(causal, jnp.exp(jnp.where(causal, diff, 0.0)), 0.0)
            a_intra[hv] = (qk[j] * decay).astype(BF16)
            xm[hv] = -jnp.where(strict, (beta_c[hv] * kk[j]) * decay, 0.0)
            tinv[hv] = eye + xm[hv]
        for _ in range(int(math.log2(c)) - 1):
            for hv in v_heads:
                xm16 = xm[hv].astype(BF16)
                xm[hv] = _dot(xm16, xm16)
            for hv in v_heads:
                tinv[hv] = tinv[hv] + _dot(tinv[hv].astype(BF16), xm[hv].astype(BF16))
        uw = {}
        for hv in v_heads:
            j = hv // rep
            v = conv_silu(2 * nqk + hv)
            tinv16 = tinv[hv].astype(BF16)
            vb16 = (v * beta_c[hv]).astype(BF16)
            kbg16 = (k[j] * (beta_c[hv] * jnp.exp(gc_c[hv]))).astype(BF16)
            if stack_state_rows:
                uw[hv] = _dot(tinv16, jnp.concatenate([vb16, kbg16], axis=1))
            else:
                uw[hv] = jnp.concatenate([_dot(tinv16, vb16), _dot(tinv16, kbg16)], axis=1)
        v_new16, o_inter = {}, {}
        for hv in v_heads:
            j = hv // rep
            s16 = s_ref[0, hv].astype(BF16)
            w16 = uw[hv][:, LANE:].astype(BF16)
            qg16 = (q[j] * jnp.exp(gc_c[hv])).astype(BF16)
            if stack_state_rows:
                ws_qs = _dot(jnp.concatenate([w16, qg16], axis=0), s16)
                ws, o_inter[hv] = ws_qs[:c], ws_qs[c:]
            else:
                ws, o_inter[hv] = _dot(w16, s16), _dot(qg16, s16)
            v_new16[hv] = (uw[hv][:, :LANE] - ws).astype(BF16)
        for hv in v_heads:
            j = hv // rep
            gl = gc_c[hv][c - 1:c, :]
            o = o_inter[hv] + _dot(a_intra[hv], v_new16[hv])
            k_dec = k[j] * jnp.exp(gl - gc_c[hv])
            s_ref[0, hv] = s_ref[0, hv] * jnp.exp(gl) + _dot_tn(k_dec.astype(BF16), v_new16[hv])
            z = p_ref[2 * nqk + GDN_V_HEADS + hv]
            o_ref[hv] = (_rms(o, gain) * _silu(z)).astype(o_ref.dtype)

    for j0 in range(0, nqk, heads_per_group):
        head_group(range(j0, j0 + heads_per_group))


def gdn_core(p, gates, conv_w_hm, gate_params, o_gain, tail0, state0, *, batch, seq_len, heads_per_group=GDN_QK_HEADS):
    nblk, m, _ = p.shape
    c = min(GDN_CHUNK, seq_len)
    assert seq_len % c == 0 and c % SUBLANE == 0 and m == batch * seq_len
    nc = seq_len // c
    ncols = 2 * GDN_QK_HEADS + GDN_V_HEADS
    o_dtype = BF16 if c % (2 * SUBLANE) == 0 else F32
    return pl.pallas_call(
        functools.partial(_gdn_kernel, chunk=c, heads_per_group=heads_per_group),
        out_shape=(jax.ShapeDtypeStruct((GDN_V_HEADS, m, LANE), o_dtype),
                   jax.ShapeDtypeStruct((batch, GDN_V_HEADS, LANE, LANE), F32)),
        grid=(batch, nc),
        in_specs=[
            pl.BlockSpec((nblk, c, LANE), lambda b, n: (0, b * nc + n, 0)),
            pl.BlockSpec((c, 2 * LANE), lambda b, n: (b * nc + n, 0)),
            pl.BlockSpec((ncols, SUBLANE, LANE), lambda b, n: (0, 0, 0)),
            pl.BlockSpec((SUBLANE, LANE), lambda b, n: (0, 0)),
            pl.BlockSpec((1, LANE), lambda b, n: (0, 0)),
            pl.BlockSpec((1, ncols, SUBLANE, LANE), lambda b, n: (b, 0, 0, 0)),
            pl.BlockSpec((1, GDN_V_HEADS, LANE, LANE), lambda b, n: (b, 0, 0, 0)),
        ],
        out_specs=(pl.BlockSpec((GDN_V_HEADS, c, LANE), lambda b, n: (0, b * nc + n, 0)),
                   pl.BlockSpec((1, GDN_V_HEADS, LANE, LANE), lambda b, n: (b, 0, 0, 0))),
        scratch_shapes=[pltpu.VMEM((ncols, SUBLANE + c, LANE), F32), pltpu.VMEM((LANE, c), F32)],
        compiler_params=_params("parallel", "arbitrary"),
        name="gdn_core",
    )(p, gates, conv_w_hm, gate_params, o_gain.reshape(1, LANE), tail0, state0)


def _rope(x, cos_f, sin_f):
    return x * cos_f + pltpu.roll(x, LANE // 2, axis=1) * sin_f


def _swa_prompt_kernel(qkv_ref, cos_ref, sin_ref, o_ref, kr_ref, qs_ref, lse_ref, *, seq_len):
    g = pl.program_id(2)
    cos_f = cos_ref[...]
    sin_f = sin_ref[...]
    qs_ref[...] = _rope(qkv_ref[0, 0, 0], cos_f, sin_f)
    kr_ref[0, 0] = _rope(qkv_ref[1, 0, 0], cos_f, sin_f)
    blk = SWA_SPAN
    ii = lax.broadcasted_iota(jnp.int32, (blk, 2 * blk), 0)
    jj = lax.broadcasted_iota(jnp.int32, (blk, 2 * blk), 1)
    dist = ii + blk - jj
    band = (dist >= 0) & (dist <= SWA_SPAN)
    scale = LANE ** -0.5
    units_per_iter = 4
    assert (seq_len // blk) % units_per_iter == 0

    for gi, (window, dil) in enumerate(SWA_GROUPS):
        assert window // dil == SWA_SPAN and seq_len % (blk * dil) == 0
        nblk = seq_len // dil // blk

        @pl.when(g == gi)
        def _(gi=gi, dil=dil, nblk=nblk):
            def units(it, carry):
                cur, first_key, q, kcat, vcat = [], [], [], [], []
                for i in range(units_per_iter):
                    u = it * units_per_iter + i
                    r = u // nblk
                    b = u % nblk
                    cur.append(pl.ds(r + b * blk * dil, blk, stride=dil))
                    prv = pl.ds(r + jnp.maximum(b - 1, 0) * blk * dil, blk, stride=dil)
                    first_key.append(jnp.where(b > 0, 0, blk))
                    q.append(qs_ref[cur[i], :].astype(BF16))
                    kcat.append(jnp.concatenate([kr_ref[0, 0, prv, :], kr_ref[0, 0, cur[i], :]], axis=0).astype(BF16))
                    vcat.append(jnp.concatenate([qkv_ref[2, 0, 0, prv, :], qkv_ref[2, 0, 0, cur[i], :]],
                                                axis=0).astype(BF16))
                s = [_dot_nt(q[i], kcat[i]) * scale for i in range(units_per_iter)]
                p = []
                for i in range(units_per_iter):
                    si = jnp.where(band & (jj >= first_key[i]), s[i], -jnp.inf)
                    mx = jnp.max(si, axis=-1, keepdims=True)
                    e = jnp.exp(si - mx)
                    den = jnp.sum(e, axis=-1, keepdims=True)
                    p.append((e * (1.0 / den)).astype(BF16))
                    lse_ref[gi, cur[i], :] = jnp.broadcast_to(mx + jnp.log(den), (blk, LANE))
                for i in range(units_per_iter):
                    o_ref[gi, 0, cur[i], :] = _dot(p[i], vcat[i])
                return carry

            lax.fori_loop(0, seq_len // blk // units_per_iter, units, 0)

    @pl.when(g == len(SWA_GROUPS) - 1)
    def _():
        lses = [lse_ref[gi] for gi in range(len(SWA_GROUPS))]
        mx = functools.reduce(jnp.maximum, lses)
        es = [jnp.exp(l - mx) for l in lses]
        den = functools.reduce(lambda a, b: a + b, es)
        for gi in range(len(SWA_GROUPS)):
            o_ref[gi, 0] = o_ref[gi, 0] * (es[gi] / den)


def swa_prompt(qkv_hm, cos_f, sin_f, *, batch, seq_len):
    ng = len(SWA_GROUPS)
    m = batch * seq_len
    qkv5 = qkv_hm.reshape(3, ng, SWA_HPG, m, LANE)
    o, kr = pl.pallas_call(
        functools.partial(_swa_prompt_kernel, seq_len=seq_len),
        out_shape=(jax.ShapeDtypeStruct((ng, SWA_HPG, m, LANE), F32),
                   jax.ShapeDtypeStruct((ng, SWA_HPG, m, LANE), F32)),
        grid=(batch, SWA_HPG, ng),
        in_specs=[
            pl.BlockSpec((3, 1, 1, seq_len, LANE), lambda b, h, g: (0, g, h, b, 0)),
            pl.BlockSpec((seq_len, LANE), lambda b, h, g: (0, 0)),
            pl.BlockSpec((seq_len, LANE), lambda b, h, g: (0, 0)),
        ],
        out_specs=(pl.BlockSpec((ng, 1, seq_len, LANE), lambda b, h, g: (0, h, b, 0)),
                   pl.BlockSpec((1, 1, seq_len, LANE), lambda b, h, g: (g, h, b, 0))),
        scratch_shapes=[pltpu.VMEM((seq_len, LANE), F32), pltpu.VMEM((ng, seq_len, LANE), F32)],
        compiler_params=_params("parallel", "parallel", "arbitrary"),
        name="swa_prompt",
    )(qkv5, cos_f, sin_f)
    return o.reshape(ng * SWA_HPG, m, LANE), kr


def _kv_pack_kernel(k_ref, v_ref, o_ref):
    nh, rows, _ = k_ref.shape[1:]
    for h in range(nh):
        o_ref[0, pl.ds(2 * h, rows, stride=2 * nh), :] = k_ref[0, h]
        o_ref[0, pl.ds(2 * h + 1, rows, stride=2 * nh), :] = v_ref[h]


def kv_pack(k_rot, qkv_hm, group, *, batch, seq_len, keep):
    ng, nh = k_rot.shape[:2]
    rows = min(keep, 512)
    assert keep % rows == 0 and (seq_len - keep) % rows == 0
    first = (seq_len - keep) // rows
    per_seq = seq_len // rows
    return pl.pallas_call(
        _kv_pack_kernel,
        out_shape=jax.ShapeDtypeStruct((batch, keep * nh * 2, LANE), F32),
        grid=(batch, keep // rows),
        in_specs=[
            pl.BlockSpec((1, nh, rows, LANE), lambda b, c: (group, 0, b * per_seq + first + c, 0)),
            pl.BlockSpec((nh, rows, LANE), lambda b, c: (2 * ng + group, b * per_seq + first + c, 0)),
        ],
        out_specs=pl.BlockSpec((1, rows * nh * 2, LANE), lambda b, c: (b, c, 0)),
        compiler_params=_params("parallel", "parallel"),
        name="kv_pack",
    )(k_rot, qkv_hm)


def _mem_prompt_kernel(q_ref, k_ref, v_ref, o_ref):
    s = _dot_nt(q_ref[0].astype(BF16), k_ref[0].astype(BF16)) * (LANE ** -0.5)
    mx = jnp.max(s, axis=-1, keepdims=True)
    e = jnp.exp(s - mx)
    p = e / jnp.sum(e, axis=-1, keepdims=True)
    o_ref[0] = _dot(p.astype(BF16), v_ref[0].astype(BF16))


def mem_prompt(q_hm, kv_hm, *, batch, seq_len, mem_len, tq):
    m = batch * seq_len
    nq = seq_len // tq
    return pl.pallas_call(
        _mem_prompt_kernel,
        out_shape=jax.ShapeDtypeStruct((MEM_HEADS, m, LANE), F32),
        grid=(batch, MEM_HEADS, nq),
        in_specs=[
            pl.BlockSpec((1, tq, LANE), lambda b, h, i: (h, b * nq + i, 0)),
            pl.BlockSpec((1, mem_len, LANE), lambda b, h, i: (h, b, 0)),
            pl.BlockSpec((1, mem_len, LANE), lambda b, h, i: (MEM_HEADS + h, b, 0)),
        ],
        out_specs=pl.BlockSpec((1, tq, LANE), lambda b, h, i: (h, b * nq + i, 0)),
        compiler_params=_params("parallel", "parallel", "parallel"),
        name="mem_prompt",
    )(q_hm, kv_hm, kv_hm)


def _mem_decode_kernel(q_ref, c_ref, o_ref, *, steps):
    nh = q_ref.shape[0]
    nb = c_ref.shape[1]
    mem_len = c_ref.shape[2] // (2 * nh)
    pairs = [(i, h) for i in range(nb) for h in range(nh)]
    s, v = {}, {}
    for i, h in pairs:
        k = c_ref[0, i, pl.ds(h, mem_len, stride=2 * nh), :].astype(BF16)
        v[i, h] = c_ref[0, i, pl.ds(nh + h, mem_len, stride=2 * nh), :].astype(BF16)
        q = q_ref[h, i * steps:(i + 1) * steps, :].astype(BF16)
        s[i, h] = _dot_nt(q, k) * (LANE ** -0.5)
    p = {}
    for key in pairs:
        e = jnp.exp(s[key] - jnp.max(s[key], axis=-1, keepdims=True))
        p[key] = (e / jnp.sum(e, axis=-1, keepdims=True)).astype(BF16)
    for i, h in pairs:
        o_ref[h, i * steps:(i + 1) * steps, :] = _dot(p[i, h], v[i, h])


def mem_decode(q_hm, cache, layer, *, batch, steps, batch_per_step=4):
    h = q_hm.shape[0]
    nb = batch_per_step if batch % batch_per_step == 0 else 1
    return pl.pallas_call(
        functools.partial(_mem_decode_kernel, steps=steps),
        out_shape=jax.ShapeDtypeStruct(q_hm.shape, F32),
        grid=(batch // nb,),
        in_specs=[
            pl.BlockSpec((h, nb * steps, LANE), lambda b: (0, b, 0)),
            pl.BlockSpec((1, nb, cache.shape[2], LANE), lambda b: (layer, b, 0, 0)),
        ],
        out_specs=pl.BlockSpec((h, nb * steps, LANE), lambda b: (0, b, 0)),
        compiler_params=_params("parallel"),
        name="mem_decode",
    )(q_hm, cache)


def _swa_decode_kernel(qkv_ref, cos_ref, sin_ref, c0_ref, c1_ref, c2_ref,
                       o_ref, r0_ref, r1_ref, r2_ref, n0_ref, n1_ref, n2_ref, sem, *, steps):
    b = pl.program_id(0)
    ng = len(SWA_GROUPS)
    caches = (c0_ref, c1_ref, c2_ref)
    rolled = (r0_ref, r1_ref, r2_ref)
    staged = (n0_ref, n1_ref, n2_ref)
    rows_per_t = 2 * SWA_HPG
    scale = LANE ** -0.5

    def roll_copy(g):
        keep = caches[g].shape[1] - 1
        return pltpu.make_async_copy(caches[g].at[0, pl.ds(1, keep)], rolled[g].at[b, pl.ds(0, keep)], sem.at[g, 0])

    def new_copy(g):
        return pltpu.make_async_copy(staged[g], rolled[g].at[b, caches[g].shape[1] - 1], sem.at[g, 1])

    for g in range(ng):
        roll_copy(g).start()

    cos_f = cos_ref[...]
    sin_f = sin_ref[...]
    u_col = lax.broadcasted_iota(jnp.int32, (steps, 1), 0)
    outs, lses = {}, {}
    for g, (window, dil) in enumerate(SWA_GROUPS):
        chunk_stride = max(dil // steps, 1)
        n_chunks = caches[g].shape[1] // chunk_stride
        shifts = list(range(0, steps, dil)) if dil < steps else [0]
        c_idx = lax.broadcasted_iota(jnp.int32, (n_chunks, steps, 1), 0)
        u_idx = lax.broadcasted_iota(jnp.int32, (n_chunks, steps, 1), 1)
        tau = c_idx * (steps * chunk_stride) + u_idx
        for h in range(SWA_HPG):
            hd = g * SWA_HPG + h
            q = _rope(qkv_ref[hd], cos_f, sin_f)
            k_new = _rope(qkv_ref[ng * SWA_HPG + hd], cos_f, sin_f)
            v_new = qkv_ref[2 * ng * SWA_HPG + hd]
            staged[g][pl.ds(2 * h, steps, stride=rows_per_t), :] = k_new
            staged[g][pl.ds(2 * h + 1, steps, stride=rows_per_t), :] = v_new
            k_c = caches[g][0, pl.ds(0, n_chunks, stride=chunk_stride), pl.ds(2 * h, steps, stride=rows_per_t), :]
            v_c = caches[g][0, pl.ds(0, n_chunks, stride=chunk_stride), pl.ds(2 * h + 1, steps, stride=rows_per_t), :]
            parts = []
            for a in shifts:
                q_a = q if a == 0 else pltpu.roll(q, steps - a, axis=0)
                sc = jnp.sum(k_c * q_a[None], axis=-1, keepdims=True) * scale
                sn = jnp.sum(k_new * q_a, axis=-1, keepdims=True) * scale
                if a > 0:
                    s_idx = (u_idx + a) & (steps - 1)
                    sc = jnp.where(tau >= s_idx, sc, -jnp.inf)
                    sn = jnp.where(u_col + a < steps, sn, -jnp.inf)
                mx = jnp.maximum(jnp.max(sc, axis=0), sn)
                e = jnp.exp(sc - mx[None])
                en = jnp.exp(sn - mx)
                den = jnp.sum(e, axis=0) + en
                acc = jnp.sum(e * v_c, axis=0) + en * v_new
                mx = jnp.broadcast_to(mx, (steps, LANE))
                den = jnp.broadcast_to(den, (steps, LANE))
                if a > 0:
                    mx, den, acc = (pltpu.roll(x, a, axis=0) for x in (mx, den, acc))
                parts.append((mx, den, acc))
            mx = functools.reduce(jnp.maximum, [p[0] for p in parts])
            ws = [jnp.exp(p[0] - mx) for p in parts]
            den = functools.reduce(lambda x, y: x + y, [w * p[1] for w, p in zip(ws, parts)])
            acc = functools.reduce(lambda x, y: x + y, [w * p[2] for w, p in zip(ws, parts)])
            outs[g, h] = acc / den
            lses[g, h] = mx + jnp.log(den)

    for g in range(ng):
        new_copy(g).start()

    for h in range(SWA_HPG):
        mx = functools.reduce(jnp.maximum, [lses[g, h] for g in range(ng)])
        es = [jnp.exp(lses[g, h] - mx) for g in range(ng)]
        den = functools.reduce(lambda x, y: x + y, es)
        for g in range(ng):
            o_ref[g * SWA_HPG + h] = outs[g, h] * (es[g] / den)

    for g in range(ng):
        roll_copy(g).wait()
        new_copy(g).wait()


def swa_decode(qkv_hm, cos_f, sin_f, caches, *, batch, steps):
    ng = len(SWA_GROUPS)
    assert steps & (steps - 1) == 0 and steps % SUBLANE == 0
    specs = []
    for (window, dil), c in zip(SWA_GROUPS, caches):
        assert c.shape[1] * steps == window and window // dil == SWA_SPAN, "cache must hold exactly one window"
        assert dil % steps == 0 or steps % dil == 0
        specs.append(pl.BlockSpec((1,) + c.shape[1:], lambda b: (b, 0, 0, 0)))
    any_spec = pl.BlockSpec(memory_space=pl.ANY)
    out = pl.pallas_call(
        functools.partial(_swa_decode_kernel, steps=steps),
        out_shape=(jax.ShapeDtypeStruct((ng * SWA_HPG, batch * steps, LANE), F32),)
        + tuple(jax.ShapeDtypeStruct(c.shape, F32) for c in caches),
        grid=(batch,),
        in_specs=[
            pl.BlockSpec((3 * ng * SWA_HPG, steps, LANE), lambda b: (0, b, 0)),
            pl.BlockSpec((steps, LANE), lambda b: (0, 0)),
            pl.BlockSpec((steps, LANE), lambda b: (0, 0)),
        ] + specs,
        out_specs=(pl.BlockSpec((ng * SWA_HPG, steps, LANE), lambda b: (0, b, 0)),) + (any_spec,) * ng,
        scratch_shapes=[pltpu.VMEM((steps * SWA_HPG * 2, LANE), F32)] * ng + [pltpu.SemaphoreType.DMA((ng, 2))],
        compiler_params=_params("arbitrary"),
        name="swa_decode",
    )(qkv_hm, cos_f, sin_f, *caches)
    return out[0], out[1:]


def _rope_tables(pos):
    half = LANE // 2
    inv = ROPE_THETA ** (-jnp.arange(half, dtype=F32) / half)
    ang = pos.astype(F32)[:, None] * inv[None, :]
    cos, sin = jnp.cos(ang), jnp.sin(ang)
    return jnp.concatenate([cos, cos], axis=1), jnp.concatenate([-sin, sin], axis=1)


def _tile_m(m):
    return 1024 if m % 1024 == 0 else m


def kernel(x_prompt, x_sample, state_gdn, state_gdn_conv, cache_swa_kv0, cache_swa_kv1, cache_swa_kv2,
           cache_mem_kv, state_ffn_conv, mem_prompt_in, norm_gains, gdn_w_in, gdn_conv_w, gdn_a_log,
           gdn_dt_bias, gdn_norm, gdn_w_out, swa_w_qkv, swa_w_out, mem_w_q, mem_w_kv, mem_w_o,
           ffn_w_up, ffn_conv_w, ffn_conv_b, ffn_w_down):
    depth = norm_gains.shape[0]
    d_model = x_prompt.shape[-1]
    qk_dim = GDN_QK_HEADS * LANE
    v_dim = GDN_V_HEADS * LANE
    conv_dim = 2 * qk_dim + v_dim
    main_dim = conv_dim + v_dim
    ncols = conv_dim // LANE

    gdn_w_out16, swa_w_out16, mem_w_o16, ffn_w_down16 = (
        w.astype(BF16) for w in (gdn_w_out, swa_w_out, mem_w_o, ffn_w_down))
    gdn_w_in_t = jnp.swapaxes(gdn_w_in, 1, 2).astype(BF16)
    swa_w_qkv16 = swa_w_qkv.astype(BF16)

    def gdn_weights(ia):
        w_tail = gdn_w_in[ia, :, main_dim:]
        w_gate = jnp.zeros((1, d_model, 2 * LANE), F32)
        w_gate = w_gate.at[0, :, :GDN_V_HEADS].set(w_tail[:, :GDN_V_HEADS])
        w_gate = w_gate.at[0, :, LANE:LANE + GDN_V_HEADS].set(w_tail[:, GDN_V_HEADS:])
        cw = gdn_conv_w[ia].reshape(GDN_CONV, ncols, LANE).transpose(1, 0, 2)
        cw = jnp.pad(cw, ((0, 0), (0, SUBLANE - GDN_CONV), (0, 0)))
        gp = jnp.zeros((SUBLANE, LANE), F32)
        gp = gp.at[0, :GDN_V_HEADS].set(gdn_a_log[ia]).at[1, :GDN_V_HEADS].set(gdn_dt_bias[ia])
        return w_gate, cw, gp, gdn_norm[ia]

    def run_trunk(x3, pos, prompt, mem_src):
        batch, seq_len, _ = x3.shape
        m = batch * seq_len
        x = x3.reshape(m, d_model)
        tm1 = _tile_m(m)
        tm2 = _tile_m(m)
        cos_f, sin_f = _rope_tables(pos)
        out = {"gdn_s": [], "gdn_c": [], "swa": [[] for _ in SWA_GROUPS], "ffn": [], "mem_kv": []}
        ia = ib = 0
        for layer in range(depth):
            g = norm_gains[layer]
            if layer % 2 == 0:
                w_gate, cw, gp, o_gain = gdn_weights(ia)
                p = norm_matmul(x, g[0], gdn_w_in_t, ia, tm=tm1, tn=512, n=main_dim, head_major=True,
                                w_transposed=True)
                gates = norm_matmul(x, g[0], w_gate, 0, tm=tm1, tn=2 * LANE)
                if prompt:
                    tail0 = jnp.zeros((batch, ncols, SUBLANE, LANE), F32)
                    s0 = jnp.zeros((batch, GDN_V_HEADS, LANE, LANE), F32)
                else:
                    cb = state_gdn_conv[ia].reshape(batch, GDN_CONV - 1, ncols, LANE).transpose(0, 2, 1, 3)
                    tail0 = jnp.pad(cb, ((0, 0), (0, 0), (SUBLANE - (GDN_CONV - 1), 0), (0, 0)))
                    s0 = state_gdn[ia]
                o_hm, s1 = gdn_core(p, gates, cw, gp, o_gain, tail0, s0, batch=batch, seq_len=seq_len)
                out["gdn_s"].append(s1)
                c1 = p.reshape(-1, batch, seq_len, LANE)[:ncols, :, seq_len - (GDN_CONV - 1):]
                out["gdn_c"].append(c1.transpose(1, 2, 0, 3).reshape(batch, GDN_CONV - 1, conv_dim))
                x = matmul_resnorm(o_hm, gdn_w_out16, ia, x, g[1], tm=tm2, tk=1024, head_major=True)
                ia += 1
            else:
                ng = len(SWA_GROUPS)
                qkv_hm = norm_matmul(x, g[0], swa_w_qkv16, ib, tm=tm1, tn=768, head_major=True)
                if prompt:
                    o_hm, kr = swa_prompt(qkv_hm, cos_f, sin_f, batch=batch, seq_len=seq_len)
                    for gi, (window, _) in enumerate(SWA_GROUPS):
                        keep = min(window, seq_len)
                        kv = kv_pack(kr, qkv_hm, gi, batch=batch, seq_len=seq_len, keep=keep)
                        out["swa"][gi].append(kv.reshape(batch, keep, SWA_HPG, 2, LANE).transpose(0, 1, 3, 2, 4))
                else:
                    caches = [c[ib].transpose(0, 1, 3, 2, 4).reshape(batch, -1, seq_len * SWA_HPG * 2, LANE)
                              for c in (cache_swa_kv0, cache_swa_kv1, cache_swa_kv2)]
                    o_hm, rolled = swa_decode(qkv_hm, cos_f, sin_f, caches, batch=batch, steps=seq_len)
                    for gi in range(ng):
                        r5 = rolled[gi].reshape(batch, -1, SWA_HPG, 2, LANE)
                        out["swa"][gi].append(r5.transpose(0, 1, 3, 2, 4))
                x = matmul_resnorm(o_hm, swa_w_out16, ib, x, g[1], tm=tm2, tk=768, head_major=True)
                ib += 1
            q_hm = norm_matmul(x, g[2], mem_w_q, layer, tm=tm1, tn=512, head_major=True)
            if prompt:
                mem_len = mem_src.shape[1]
                kv_hm = norm_matmul(mem_src.reshape(batch * mem_len, d_model), g[2], mem_w_kv, layer,
                                    tm=_tile_m(batch * mem_len), tn=512, do_norm=False, head_major=True)
                out["mem_kv"].append(kv_hm.reshape(2, MEM_HEADS, batch, mem_len, LANE).transpose(2, 3, 0, 1, 4))
                o_hm = mem_prompt(q_hm, kv_hm, batch=batch, seq_len=seq_len, mem_len=mem_len, tq=1024)
            else:
                mem_rows = mem_src.reshape(mem_src.shape[0], batch, -1, LANE)
                o_hm = mem_decode(q_hm, mem_rows, layer, batch=batch, steps=seq_len)
            x = matmul_resnorm(o_hm, mem_w_o16, layer, x, g[3], tm=tm2, tk=512, head_major=True)
            if prompt:
                act, gt = ffn_up(x, g[4], ffn_w_up, layer, ffn_conv_w[layer], ffn_conv_b[layer], seq_len=seq_len,
                                 tm=tm1, tn=512)
                per_seq = seq_len // tm1
                out["ffn"].append(gt[per_seq - 1::per_seq, SUBLANE - (FFN_CONV - 1):])
            else:
                buf = state_ffn_conv[layer]
                zero = jnp.zeros((batch, seq_len - 1, buf.shape[-1]), F32)
                hist1 = jnp.concatenate([buf[:, 1:2], zero], axis=1).reshape(m, -1)
                hist2 = jnp.concatenate([buf, zero[:, 1:]], axis=1).reshape(m, -1)
                act, gt = ffn_up(x, g[4], ffn_w_up, layer, ffn_conv_w[layer], ffn_conv_b[layer], seq_len=seq_len,
                                 tm=tm1, tn=512, hist=(hist1, hist2))
                out["ffn"].append(gt.reshape(batch, seq_len, -1)[:, seq_len - (FFN_CONV - 1):])
            x = matmul_resnorm(act, ffn_w_down16, layer, x, g[5], tm=tm2, tk=512, head_major=False)
        return x.reshape(batch, seq_len, d_model), out

    pos_p = jnp.arange(x_prompt.shape[1])
    y_p, op = run_trunk(x_prompt, pos_p, True, mem_prompt_in)
    pos_s = PAST_LEN + jnp.arange(x_sample.shape[1])
    y_s, os_ = run_trunk(x_sample, pos_s, False, cache_mem_kv)

    return (y_p, y_s, jnp.stack(op["gdn_s"]), jnp.stack(op["gdn_c"]),
            jnp.stack(op["swa"][0]), jnp.stack(op["swa"][1]), jnp.stack(op["swa"][2]),
            jnp.stack(op["mem_kv"]), jnp.stack(op["ffn"]),
            jnp.stack(os_["gdn_s"]), jnp.stack(os_["gdn_c"]),
            jnp.stack(os_["swa"][0]), jnp.stack(os_["swa"][1]), jnp.stack(os_["swa"][2]),
            jnp.stack(os_["ffn"]))
```

```python
import functools
import math

import jax
import jax.numpy as jnp
from jax import lax
from jax.experimental import pallas as pl
from jax.experimental.pallas import tpu as pltpu

F32 = jnp.float32
BF16 = jnp.bfloat16
HIGHEST = lax.Precision.HIGHEST

LANE = 128
SUBLANE = 8
NORM_EPS = 1e-6
ROPE_THETA = 10000.0
PAST_LEN = 2048
GDN_CHUNK = 64
GDN_QK_HEADS = 16
GDN_V_HEADS = 32
GDN_CONV = 4
SWA_GROUPS = ((128, 1), (512, 4), (2048, 16))
SWA_HPG = 6
SWA_SPAN = 128
MEM_HEADS = 4
FFN_CONV = 3
VMEM_LIMIT_BYTES = 52 * 1024 * 1024
MRN_COL_CHUNK = 512


ROW_TILE = 1024
STREAM_BLOCK_BYTES = int(4.75 * 1024 * 1024)
PAIR_BLOCK_BYTES = int(8.5 * 1024 * 1024)


def _row_tile(m):
    return ROW_TILE if m % ROW_TILE == 0 else m


def _lane_tile(dim, bytes_per_unit, budget):
    fits = [t for t in range(LANE, dim + 1, LANE) if dim % t == 0 and t * bytes_per_unit <= budget]
    return max(fits) if fits else LANE


def _params(*semantics):
    return pltpu.CompilerParams(dimension_semantics=semantics, vmem_limit_bytes=VMEM_LIMIT_BYTES)


def _dot(a, b):
    return jnp.dot(a, b, preferred_element_type=F32)


def _dot_nt(a, b):
    return lax.dot_general(a, b, (((1,), (1,)), ((), ())), preferred_element_type=F32)


def _dot_tn(a, b):
    return lax.dot_general(a, b, (((0,), (0,)), ((), ())), preferred_element_type=F32)


def _rms(x, gain):
    return x * lax.rsqrt(jnp.mean(x * x, axis=-1, keepdims=True) + NORM_EPS) * gain


def _silu(x):
    return x * jax.nn.sigmoid(x)


def _nmm_kernel(x_ref, g_ref, w_ref, o_ref, h_ref, *, do_norm, head_major, w_transposed):
    @pl.when(pl.program_id(1) == 0)
    def _():
        x = x_ref[...]
        if do_norm:
            x = _rms(x, g_ref[...])
        h_ref[...] = x.astype(BF16)

    if w_transposed:
        acc = _dot_nt(h_ref[...], w_ref[0].astype(BF16))
    else:
        acc = _dot(h_ref[...], w_ref[0].astype(BF16))
    if head_major:
        for c in range(o_ref.shape[0]):
            o_ref[c] = acc[:, c * LANE:(c + 1) * LANE]
    else:
        o_ref[...] = acc


def norm_matmul(x, gain, w, layer, *, n=None, do_norm=True, head_major=False, w_transposed=False):
    m, k = x.shape
    n = w.shape[1 if w_transposed else 2] if n is None else n
    tm = _row_tile(m)
    tn = _lane_tile(n, k * w.dtype.itemsize, STREAM_BLOCK_BYTES)
    assert m % tm == 0 and n % tn == 0 and tn % LANE == 0
    if w_transposed:
        w_spec = pl.BlockSpec((1, tn, k), lambda i, j: (layer, j, 0))
    else:
        w_spec = pl.BlockSpec((1, k, tn), lambda i, j: (layer, 0, j))
    if head_major:
        out_shape = jax.ShapeDtypeStruct((n // LANE, m, LANE), F32)
        out_spec = pl.BlockSpec((tn // LANE, tm, LANE), lambda i, j: (j, i, 0))
    else:
        out_shape = jax.ShapeDtypeStruct((m, n), F32)
        out_spec = pl.BlockSpec((tm, tn), lambda i, j: (i, j))
    return pl.pallas_call(
        functools.partial(_nmm_kernel, do_norm=do_norm, head_major=head_major, w_transposed=w_transposed),
        out_shape=out_shape,
        grid=(m // tm, n // tn),
        in_specs=[
            pl.BlockSpec((tm, k), lambda i, j: (i, 0)),
            pl.BlockSpec((1, k), lambda i, j: (0, 0)),
            w_spec,
        ],
        out_specs=out_spec,
        scratch_shapes=[pltpu.VMEM((tm, k), BF16)],
        compiler_params=_params("parallel", "arbitrary"),
        name="norm_matmul",
    )(x, gain.reshape(1, k), w)


def _mrn_kernel(a_ref, w_ref, x_ref, g_ref, o_ref, *, head_major):
    k = pl.program_id(1)
    if head_major:
        a = jnp.concatenate([a_ref[c].astype(BF16) for c in range(a_ref.shape[0])], axis=1)
    else:
        a = a_ref[...].astype(BF16)
    d = o_ref.shape[1]
    cols = [slice(c, c + MRN_COL_CHUNK) for c in range(0, d, MRN_COL_CHUNK)]
    @pl.when(k == 0)
    def _():
        o_ref[...] = jnp.zeros_like(o_ref)

    for cs in cols:
        o_ref[:, cs] += _dot(a, w_ref[0, :, cs])

    @pl.when(k == pl.num_programs(1) - 1)
    def _():
        ss = functools.reduce(lambda x, y: x + y,
                              [jnp.sum(o_ref[:, cs] * o_ref[:, cs], axis=-1, keepdims=True) for cs in cols])
        inv = lax.rsqrt(ss * (1.0 / d) + NORM_EPS)
        for cs in cols:
            o_ref[:, cs] = x_ref[:, cs] + (o_ref[:, cs] * inv) * g_ref[:, cs]


def matmul_resnorm(a, w, layer, x, gain, *, head_major):
    m, d = x.shape
    kdim = w.shape[1]
    tm = _row_tile(m)
    tk = _lane_tile(kdim, d * w.dtype.itemsize + tm * a.dtype.itemsize, PAIR_BLOCK_BYTES)
    assert m % tm == 0 and kdim % tk == 0 and tk % LANE == 0
    if head_major:
        a_spec = pl.BlockSpec((tk // LANE, tm, LANE), lambda i, k: (k, i, 0))
    else:
        a_spec = pl.BlockSpec((tm, tk), lambda i, k: (i, k))
    return pl.pallas_call(
        functools.partial(_mrn_kernel, head_major=head_major),
        out_shape=jax.ShapeDtypeStruct((m, d), F32),
        grid=(m // tm, kdim // tk),
        in_specs=[
            a_spec,
            pl.BlockSpec((1, tk, d), lambda i, k: (layer, k, 0)),
            pl.BlockSpec((tm, d), lambda i, k: (i, 0)),
            pl.BlockSpec((1, d), lambda i, k: (0, 0)),
        ],
        out_specs=pl.BlockSpec((tm, d), lambda i, k: (i, 0)),
        compiler_params=_params("parallel", "arbitrary"),
        name="matmul_resnorm",
    )(a, w, x, gain.reshape(1, d))


def _gelu_tanh(x):
    c = math.sqrt(2.0 / math.pi)
    return x * (0.5 * (1.0 + jnp.tanh(c * (x + 0.044715 * (x * x * x)))))


def _ffn_up_kernel(*refs, seq_len, tm, long_seq):
    if long_seq:
        x_ref, g_ref, wg_ref, wu_ref, cw_ref, cb_ref, act_ref, gt_ref, h_ref, carry_ref = refs
    else:
        x_ref, g_ref, wg_ref, wu_ref, cw_ref, cb_ref, h1_ref, h2_ref, act_ref, gt_ref, h_ref = refs
    i = pl.program_id(0)
    j = pl.program_id(1)

    @pl.when(j == 0)
    def _():
        h_ref[...] = _rms(x_ref[...], g_ref[...]).astype(BF16)

    h = h_ref[...]
    gate = _dot(h, wg_ref[0].astype(BF16))
    up = _dot(h, wu_ref[0].astype(BF16))
    row = lax.broadcasted_iota(jnp.int32, gate.shape, 0)
    sh1 = pltpu.roll(gate, 1, axis=0)
    sh2 = pltpu.roll(gate, 2, axis=0)
    if long_seq:
        prev = jnp.where((i * tm) % seq_len == 0, 0.0, carry_ref[j])
        carry_ref[j] = gate[tm - SUBLANE:tm]
        gt_ref[0] = gate[tm - SUBLANE:tm]
        sh1 = jnp.where(row == 0, prev[7:8], sh1)
        sh2 = jnp.where(row == 0, prev[6:7], jnp.where(row == 1, prev[7:8], sh2))
    else:
        pos = row % SUBLANE
        gt_ref[...] = gate
        sh1 = jnp.where(pos == 0, 0.0, sh1) + h1_ref[...]
        sh2 = jnp.where(pos < 2, 0.0, sh2) + h2_ref[...]
    y = sh2 * cw_ref[0:1] + sh1 * cw_ref[1:2] + gate * cw_ref[2:3]
    act_ref[...] = (_gelu_tanh(y + cb_ref[...]) * up).astype(BF16)


def ffn_up(x, gain, w_up, layer, conv_w, conv_b, *, seq_len, hist=None):
    m, k = x.shape
    f = conv_w.shape[1]
    tm = _row_tile(m)
    tn = _lane_tile(f, 2 * k * w_up.dtype.itemsize, PAIR_BLOCK_BYTES)
    nj = f // tn
    assert m % tm == 0 and f % tn == 0
    long_seq = seq_len >= tm
    if long_seq:
        assert seq_len % tm == 0
    else:
        assert seq_len == SUBLANE
    in_specs = [
        pl.BlockSpec((tm, k), lambda i, j: (i, 0)),
        pl.BlockSpec((1, k), lambda i, j: (0, 0)),
        pl.BlockSpec((1, k, tn), lambda i, j: (layer, 0, j)),
        pl.BlockSpec((1, k, tn), lambda i, j: (layer, 0, j + nj)),
        pl.BlockSpec((FFN_CONV, tn), lambda i, j: (0, j)),
        pl.BlockSpec((1, tn), lambda i, j: (0, j)),
    ]
    args = [x, gain.reshape(1, k), w_up, w_up, conv_w, conv_b.reshape(1, f)]
    scratch = [pltpu.VMEM((tm, k), BF16)]
    if long_seq:
        gt_shape = jax.ShapeDtypeStruct((m // tm, SUBLANE, f), F32)
        gt_spec = pl.BlockSpec((1, SUBLANE, tn), lambda i, j: (i, 0, j))
        scratch.append(pltpu.VMEM((nj, SUBLANE, tn), F32))
    else:
        gt_shape = jax.ShapeDtypeStruct((m, f), F32)
        gt_spec = pl.BlockSpec((tm, tn), lambda i, j: (i, j))
        in_specs += [pl.BlockSpec((tm, tn), lambda i, j: (i, j))] * 2
        args += list(hist)
    return pl.pallas_call(
        functools.partial(_ffn_up_kernel, seq_len=seq_len, tm=tm, long_seq=long_seq),
        out_shape=(jax.ShapeDtypeStruct((m, f), BF16), gt_shape),
        grid=(m // tm, nj),
        in_specs=in_specs,
        out_specs=(pl.BlockSpec((tm, tn), lambda i, j: (i, j)), gt_spec),
        scratch_shapes=scratch,
        compiler_params=_params("arbitrary", "arbitrary"),
        name="ffn_up",
    )(*args)


def _gdn_kernel(p_ref, g_ref, cw_ref, gp_ref, gain_ref, tail0_ref, s0_ref, o_ref, s_ref,
                tail_ref, gct_ref, *, chunk, heads_per_group):
    c = chunk
    nqk = GDN_QK_HEADS
    rep = GDN_V_HEADS // GDN_QK_HEADS
    stack_state_rows = 2 * c <= 2 * SUBLANE

    @pl.when(pl.program_id(1) == 0)
    def _():
        s_ref[...] = s0_ref[...]
        tail_ref[:, 0:SUBLANE] = tail0_ref[0]

    gates = g_ref[...]
    a_log = gp_ref[0:1, :]
    dt_bias = gp_ref[1:2, :]
    beta_all = jax.nn.sigmoid(gates[:, :LANE])
    xa = gates[:, LANE:] + dt_bias
    softplus = jnp.maximum(xa, 0.0) + jnp.log1p(jnp.exp(-jnp.abs(xa)))
    g_all = -jnp.exp(a_log) * softplus
    row = lax.broadcasted_iota(jnp.int32, (c, c), 0)
    col = lax.broadcasted_iota(jnp.int32, (c, c), 1)
    causal = row >= col
    strict = row > col
    eye = (row == col).astype(F32)
    gc_all = jnp.dot(causal.astype(F32), g_all, precision=HIGHEST, preferred_element_type=F32)
    gct_ref[...] = gc_all.T
    gain = gain_ref[...]

    def conv_silu(cb):
        x = p_ref[cb]
        w = cw_ref[cb]
        tail_ref[cb, SUBLANE:SUBLANE + c] = x
        y = (tail_ref[cb, pl.ds(5, c)] * w[0:1] + tail_ref[cb, pl.ds(6, c)] * w[1:2]
             + tail_ref[cb, pl.ds(7, c)] * w[2:3] + x * w[3:4])
        tail_ref[cb, 0:SUBLANE] = x[c - SUBLANE:c]
        return _silu(y)

    def l2norm(x):
        return x * lax.rsqrt(jnp.sum(x * x, axis=-1, keepdims=True) + NORM_EPS)

    def head_group(qk_heads):
        v_heads = [rep * j + e for j in qk_heads for e in range(rep)]
        q, k, kk, qk = {}, {}, {}, {}
        for j in qk_heads:
            q[j] = l2norm(conv_silu(j)) * (LANE ** -0.5)
            k[j] = l2norm(conv_silu(nqk + j))
        for j in qk_heads:
            k16 = k[j].astype(BF16)
            kk[j] = _dot_nt(k16, k16)
            qk[j] = _dot_nt(q[j].astype(BF16), k16)
        beta_c, gc_c, a_intra, xm, tinv = {}, {}, {}, {}, {}
        for hv in v_heads:
            j = hv // rep
            beta_c[hv] = beta_all[:, hv:hv + 1]
            gc_c[hv] = gc_all[:, hv:hv + 1]
            diff = gc_c[hv] - gct_ref[hv:hv + 1, :]
            decay = jnp.where(causal, jnp.exp(jnp.where(causal, diff, 0.0)), 0.0)
            a_intra[hv] = (qk[j] * decay).astype(BF16)
            xm[hv] = -jnp.where(strict, (beta_c[hv] * kk[j]) * decay, 0.0)
            tinv[hv] = eye + xm[hv]
        for _ in range(int(math.log2(c)) - 1):
            for hv in v_heads:
                xm16 = xm[hv].astype(BF16)
                xm[hv] = _dot(xm16, xm16)
            for hv in v_heads:
                tinv[hv] = tinv[hv] + _dot(tinv[hv].astype(BF16), xm[hv].astype(BF16))
        uw = {}
        for hv in v_heads:
            j = hv // rep
            v = conv_silu(2 * nqk + hv)
            tinv16 = tinv[hv].astype(BF16)
            vb16 = (v * beta_c[hv]).astype(BF16)
            kbg16 = (k[j] * (beta_c[hv] * jnp.exp(gc_c[hv]))).astype(BF16)
            if stack_state_rows:
                uw[hv] = _dot(tinv16, jnp.concatenate([vb16, kbg16], axis=1))
            else:
                uw[hv] = jnp.concatenate([_dot(tinv16, vb16), _dot(tinv16, kbg16)], axis=1)
        v_new16, o_inter = {}, {}
        for hv in v_heads:
            j = hv // rep
            s16 = s_ref[0, hv].astype(BF16)
            w16 = uw[hv][:, LANE:].astype(BF16)
            qg16 = (q[j] * jnp.exp(gc_c[hv])).astype(BF16)
            if stack_state_rows:
                ws_qs = _dot(jnp.concatenate([w16, qg16], axis=0), s16)
                ws, o_inter[hv] = ws_qs[:c], ws_qs[c:]
            else:
                ws, o_inter[hv] = _dot(w16, s16), _dot(qg16, s16)
            v_new16[hv] = (uw[hv][:, :LANE] - ws).astype(BF16)
        for hv in v_heads:
            j = hv // rep
            gl = gc_c[hv][c - 1:c, :]
            o = o_inter[hv] + _dot(a_intra[hv], v_new16[hv])
            k_dec = k[j] * jnp.exp(gl - gc_c[hv])
            s_ref[0, hv] = s_ref[0, hv] * jnp.exp(gl) + _dot_tn(k_dec.astype(BF16), v_new16[hv])
            z = p_ref[2 * nqk + GDN_V_HEADS + hv]
            o_ref[hv] = (_rms(o, gain) * _silu(z)).astype(o_ref.dtype)

    for j0 in range(0, nqk, heads_per_group):
        head_group(range(j0, j0 + heads_per_group))


def gdn_core(p, gates, conv_w_hm, gate_params, o_gain, tail0, state0, *, batch, seq_len, heads_per_group=GDN_QK_HEADS):
    nblk, m, _ = p.shape
    c = min(GDN_CHUNK, seq_len)
    assert seq_len % c == 0 and c % SUBLANE == 0 and m == batch * seq_len
    nc = seq_len // c
    ncols = 2 * GDN_QK_HEADS + GDN_V_HEADS
    o_dtype = BF16 if c % (2 * SUBLANE) == 0 else F32
    return pl.pallas_call(
        functools.partial(_gdn_kernel, chunk=c, heads_per_group=heads_per_group),
        out_shape=(jax.ShapeDtypeStruct((GDN_V_HEADS, m, LANE), o_dtype),
                   jax.ShapeDtypeStruct((batch, GDN_V_HEADS, LANE, LANE), F32)),
        grid=(batch, nc),
        in_specs=[
            pl.BlockSpec((nblk, c, LANE), lambda b, n: (0, b * nc + n, 0)),
            pl.BlockSpec((c, 2 * LANE), lambda b, n: (b * nc + n, 0)),
            pl.BlockSpec((ncols, SUBLANE, LANE), lambda b, n: (0, 0, 0)),
            pl.BlockSpec((SUBLANE, LANE), lambda b, n: (0, 0)),
            pl.BlockSpec((1, LANE), lambda b, n: (0, 0)),
            pl.BlockSpec((1, ncols, SUBLANE, LANE), lambda b, n: (b, 0, 0, 0)),
            pl.BlockSpec((1, GDN_V_HEADS, LANE, LANE), lambda b, n: (b, 0, 0, 0)),
        ],
        out_specs=(pl.BlockSpec((GDN_V_HEADS, c, LANE), lambda b, n: (0, b * nc + n, 0)),
                   pl.BlockSpec((1, GDN_V_HEADS, LANE, LANE), lambda b, n: (b, 0, 0, 0))),
        scratch_shapes=[pltpu.VMEM((ncols, SUBLANE + c, LANE), F32), pltpu.VMEM((LANE, c), F32)],
        compiler_params=_params("parallel", "arbitrary"),
        name="gdn_core",
    )(p, gates, conv_w_hm, gate_params, o_gain.reshape(1, LANE), tail0, state0)


def _rope(x, cos_f, sin_f):
    return x * cos_f + pltpu.roll(x, LANE // 2, axis=1) * sin_f


def _swa_prompt_kernel(qkv_ref, cos_ref, sin_ref, o_ref, kr_ref, qs_ref, lse_ref, *, seq_len):
    g = pl.program_id(2)
    cos_f = cos_ref[...]
    sin_f = sin_ref[...]
    qs_ref[...] = _rope(qkv_ref[0, 0, 0], cos_f, sin_f)
    kr_ref[0, 0] = _rope(qkv_ref[1, 0, 0], cos_f, sin_f)
    blk = SWA_SPAN
    ii = lax.broadcasted_iota(jnp.int32, (blk, 2 * blk), 0)
    jj = lax.broadcasted_iota(jnp.int32, (blk, 2 * blk), 1)
    dist = ii + blk - jj
    band = (dist >= 0) & (dist <= SWA_SPAN)
    scale = LANE ** -0.5
    units_per_iter = 4
    assert (seq_len // blk) % units_per_iter == 0

    for gi, (window, dil) in enumerate(SWA_GROUPS):
        assert window // dil == SWA_SPAN and seq_len % (blk * dil) == 0
        nblk = seq_len // dil // blk

        @pl.when(g == gi)
        def _(gi=gi, dil=dil, nblk=nblk):
            def units(it, carry):
                cur, first_key, q, kcat, vcat = [], [], [], [], []
                for i in range(units_per_iter):
                    u = it * units_per_iter + i
                    r = u // nblk
                    b = u % nblk
                    cur.append(pl.ds(r + b * blk * dil, blk, stride=dil))
                    prv = pl.ds(r + jnp.maximum(b - 1, 0) * blk * dil, blk, stride=dil)
                    first_key.append(jnp.where(b > 0, 0, blk))
                    q.append(qs_ref[cur[i], :].astype(BF16))
                    kcat.append(jnp.concatenate([kr_ref[0, 0, prv, :], kr_ref[0, 0, cur[i], :]], axis=0).astype(BF16))
                    vcat.append(jnp.concatenate([qkv_ref[2, 0, 0, prv, :], qkv_ref[2, 0, 0, cur[i], :]],
                                                axis=0).astype(BF16))
                s = [_dot_nt(q[i], kcat[i]) * scale for i in range(units_per_iter)]
                p = []
                for i in range(units_per_iter):
                    si = jnp.where(band & (jj >= first_key[i]), s[i], -jnp.inf)
                    mx = jnp.max(si, axis=-1, keepdims=True)
                    e = jnp.exp(si - mx)
                    den = jnp.sum(e, axis=-1, keepdims=True)
                    p.append((e * (1.0 / den)).astype(BF16))
                    lse_ref[gi, cur[i], :] = jnp.broadcast_to(mx + jnp.log(den), (blk, LANE))
                for i in range(units_per_iter):
                    o_ref[gi, 0, cur[i], :] = _dot(p[i], vcat[i])
                return carry

            lax.fori_loop(0, seq_len // blk // units_per_iter, units, 0)

    @pl.when(g == len(SWA_GROUPS) - 1)
    def _():
        lses = [lse_ref[gi] for gi in range(len(SWA_GROUPS))]
        mx = functools.reduce(jnp.maximum, lses)
        es = [jnp.exp(l - mx) for l in lses]
        den = functools.reduce(lambda a, b: a + b, es)
        for gi in range(len(SWA_GROUPS)):
            o_ref[gi, 0] = o_ref[gi, 0] * (es[gi] / den)


def swa_prompt(qkv_hm, cos_f, sin_f, *, batch, seq_len):
    ng = len(SWA_GROUPS)
    m = batch * seq_len
    qkv5 = qkv_hm.reshape(3, ng, SWA_HPG, m, LANE)
    o, kr = pl.pallas_call(
        functools.partial(_swa_prompt_kernel, seq_len=seq_len),
        out_shape=(jax.ShapeDtypeStruct((ng, SWA_HPG, m, LANE), F32),
                   jax.ShapeDtypeStruct((ng, SWA_HPG, m, LANE), F32)),
        grid=(batch, SWA_HPG, ng),
        in_specs=[
            pl.BlockSpec((3, 1, 1, seq_len, LANE), lambda b, h, g: (0, g, h, b, 0)),
            pl.BlockSpec((seq_len, LANE), lambda b, h, g: (0, 0)),
            pl.BlockSpec((seq_len, LANE), lambda b, h, g: (0, 0)),
        ],
        out_specs=(pl.BlockSpec((ng, 1, seq_len, LANE), lambda b, h, g: (0, h, b, 0)),
                   pl.BlockSpec((1, 1, seq_len, LANE), lambda b, h, g: (g, h, b, 0))),
        scratch_shapes=[pltpu.VMEM((seq_len, LANE), F32), pltpu.VMEM((ng, seq_len, LANE), F32)],
        compiler_params=_params("parallel", "parallel", "arbitrary"),
        name="swa_prompt",
    )(qkv5, cos_f, sin_f)
    return o.reshape(ng * SWA_HPG, m, LANE), kr


def _kv_pack_kernel(k_ref, v_ref, o_ref):
    nh, rows, _ = k_ref.shape[1:]
    for h in range(nh):
        o_ref[0, pl.ds(2 * h, rows, stride=2 * nh), :] = k_ref[0, h]
        o_ref[0, pl.ds(2 * h + 1, rows, stride=2 * nh), :] = v_ref[h]


def kv_pack(k_rot, qkv_hm, group, *, batch, seq_len, keep):
    ng, nh = k_rot.shape[:2]
    rows = min(keep, 512)
    assert keep % rows == 0 and (seq_len - keep) % rows == 0
    first = (seq_len - keep) // rows
    per_seq = seq_len // rows
    return pl.pallas_call(
        _kv_pack_kernel,
        out_shape=jax.ShapeDtypeStruct((batch, keep * nh * 2, LANE), F32),
        grid=(batch, keep // rows),
        in_specs=[
            pl.BlockSpec((1, nh, rows, LANE), lambda b, c: (group, 0, b * per_seq + first + c, 0)),
            pl.BlockSpec((nh, rows, LANE), lambda b, c: (2 * ng + group, b * per_seq + first + c, 0)),
        ],
        out_specs=pl.BlockSpec((1, rows * nh * 2, LANE), lambda b, c: (b, c, 0)),
        compiler_params=_params("parallel", "parallel"),
        name="kv_pack",
    )(k_rot, qkv_hm)


def _mem_prompt_kernel(q_ref, k_ref, v_ref, o_ref):
    s = _dot_nt(q_ref[0].astype(BF16), k_ref[0].astype(BF16)) * (LANE ** -0.5)
    mx = jnp.max(s, axis=-1, keepdims=True)
    e = jnp.exp(s - mx)
    p = e / jnp.sum(e, axis=-1, keepdims=True)
    o_ref[0] = _dot(p.astype(BF16), v_ref[0].astype(BF16))


def mem_prompt(q_hm, kv_hm, *, batch, seq_len, mem_len, tq):
    m = batch * seq_len
    nq = seq_len // tq
    return pl.pallas_call(
        _mem_prompt_kernel,
        out_shape=jax.ShapeDtypeStruct((MEM_HEADS, m, LANE), F32),
        grid=(batch, MEM_HEADS, nq),
        in_specs=[
            pl.BlockSpec((1, tq, LANE), lambda b, h, i: (h, b * nq + i, 0)),
            pl.BlockSpec((1, mem_len, LANE), lambda b, h, i: (h, b, 0)),
            pl.BlockSpec((1, mem_len, LANE), lambda b, h, i: (MEM_HEADS + h, b, 0)),
        ],
        out_specs=pl.BlockSpec((1, tq, LANE), lambda b, h, i: (h, b * nq + i, 0)),
        compiler_params=_params("parallel", "parallel", "parallel"),
        name="mem_prompt",
    )(q_hm, kv_hm, kv_hm)


def _mem_decode_kernel(q_ref, c_ref, o_ref, *, steps):
    nh = q_ref.shape[0]
    nb = c_ref.shape[1]
    mem_len = c_ref.shape[2] // (2 * nh)
    pairs = [(i, h) for i in range(nb) for h in range(nh)]
    s, v = {}, {}
    for i, h in pairs:
        k = c_ref[0, i, pl.ds(h, mem_len, stride=2 * nh), :].astype(BF16)
        v[i, h] = c_ref[0, i, pl.ds(nh + h, mem_len, stride=2 * nh), :].astype(BF16)
        q = q_ref[h, i * steps:(i + 1) * steps, :].astype(BF16)
        s[i, h] = _dot_nt(q, k) * (LANE ** -0.5)
    p = {}
    for key in pairs:
        e = jnp.exp(s[key] - jnp.max(s[key], axis=-1, keepdims=True))
        p[key] = (e / jnp.sum(e, axis=-1, keepdims=True)).astype(BF16)
    for i, h in pairs:
        o_ref[h, i * steps:(i + 1) * steps, :] = _dot(p[i, h], v[i, h])


def mem_decode(q_hm, cache, layer, *, batch, steps, batch_per_step=4):
    h = q_hm.shape[0]
    nb = batch_per_step if batch % batch_per_step == 0 else 1
    return pl.pallas_call(
        functools.partial(_mem_decode_kernel, steps=steps),
        out_shape=jax.ShapeDtypeStruct(q_hm.shape, F32),
        grid=(batch // nb,),
        in_specs=[
            pl.BlockSpec((h, nb * steps, LANE), lambda b: (0, b, 0)),
            pl.BlockSpec((1, nb, cache.shape[2], LANE), lambda b: (layer, b, 0, 0)),
        ],
        out_specs=pl.BlockSpec((h, nb * steps, LANE), lambda b: (0, b, 0)),
        compiler_params=_params("parallel"),
        name="mem_decode",
    )(q_hm, cache)


def _swa_decode_kernel(qkv_ref, cos_ref, sin_ref, c0_ref, c1_ref, c2_ref,
                       o_ref, r0_ref, r1_ref, r2_ref, n0_ref, n1_ref, n2_ref, sem, *, steps):
    b = pl.program_id(0)
    ng = len(SWA_GROUPS)
    caches = (c0_ref, c1_ref, c2_ref)
    rolled = (r0_ref, r1_ref, r2_ref)
    staged = (n0_ref, n1_ref, n2_ref)
    rows_per_t = 2 * SWA_HPG
    scale = LANE ** -0.5

    def roll_copy(g):
        keep = caches[g].shape[1] - 1
        return pltpu.make_async_copy(caches[g].at[0, pl.ds(1, keep)], rolled[g].at[b, pl.ds(0, keep)], sem.at[g, 0])

    def new_copy(g):
        return pltpu.make_async_copy(staged[g], rolled[g].at[b, caches[g].shape[1] - 1], sem.at[g, 1])

    for g in range(ng):
        roll_copy(g).start()

    cos_f = cos_ref[...]
    sin_f = sin_ref[...]
    u_col = lax.broadcasted_iota(jnp.int32, (steps, 1), 0)
    outs, lses = {}, {}
    for g, (window, dil) in enumerate(SWA_GROUPS):
        chunk_stride = max(dil // steps, 1)
        n_chunks = caches[g].shape[1] // chunk_stride
        shifts = list(range(0, steps, dil)) if dil < steps else [0]
        c_idx = lax.broadcasted_iota(jnp.int32, (n_chunks, steps, 1), 0)
        u_idx = lax.broadcasted_iota(jnp.int32, (n_chunks, steps, 1), 1)
        tau = c_idx * (steps * chunk_stride) + u_idx
        for h in range(SWA_HPG):
            hd = g * SWA_HPG + h
            q = _rope(qkv_ref[hd], cos_f, sin_f)
            k_new = _rope(qkv_ref[ng * SWA_HPG + hd], cos_f, sin_f)
            v_new = qkv_ref[2 * ng * SWA_HPG + hd]
            staged[g][pl.ds(2 * h, steps, stride=rows_per_t), :] = k_new
            staged[g][pl.ds(2 * h + 1, steps, stride=rows_per_t), :] = v_new
            k_c = caches[g][0, pl.ds(0, n_chunks, stride=chunk_stride), pl.ds(2 * h, steps, stride=rows_per_t), :]
            v_c = caches[g][0, pl.ds(0, n_chunks, stride=chunk_stride), pl.ds(2 * h + 1, steps, stride=rows_per_t), :]
            parts = []
            for a in shifts:
                q_a = q if a == 0 else pltpu.roll(q, steps - a, axis=0)
                sc = jnp.sum(k_c * q_a[None], axis=-1, keepdims=True) * scale
                sn = jnp.sum(k_new * q_a, axis=-1, keepdims=True) * scale
                if a > 0:
                    s_idx = (u_idx + a) & (steps - 1)
                    sc = jnp.where(tau >= s_idx, sc, -jnp.inf)
                    sn = jnp.where(u_col + a < steps, sn, -jnp.inf)
                mx = jnp.maximum(jnp.max(sc, axis=0), sn)
                e = jnp.exp(sc - mx[None])
                en = jnp.exp(sn - mx)
                den = jnp.sum(e, axis=0) + en
                acc = jnp.sum(e * v_c, axis=0) + en * v_new
                mx = jnp.broadcast_to(mx, (steps, LANE))
                den = jnp.broadcast_to(den, (steps, LANE))
                if a > 0:
                    mx, den, acc = (pltpu.roll(x, a, axis=0) for x in (mx, den, acc))
                parts.append((mx, den, acc))
            mx = functools.reduce(jnp.maximum, [p[0] for p in parts])
            ws = [jnp.exp(p[0] - mx) for p in parts]
            den = functools.reduce(lambda x, y: x + y, [w * p[1] for w, p in zip(ws, parts)])
            acc = functools.reduce(lambda x, y: x + y, [w * p[2] for w, p in zip(ws, parts)])
            outs[g, h] = acc / den
            lses[g, h] = mx + jnp.log(den)

    for g in range(ng):
        new_copy(g).start()

    for h in range(SWA_HPG):
        mx = functools.reduce(jnp.maximum, [lses[g, h] for g in range(ng)])
        es = [jnp.exp(lses[g, h] - mx) for g in range(ng)]
        den = functools.reduce(lambda x, y: x + y, es)
        for g in range(ng):
            o_ref[g * SWA_HPG + h] = outs[g, h] * (es[g] / den)

    for g in range(ng):
        roll_copy(g).wait()
        new_copy(g).wait()


def swa_decode(qkv_hm, cos_f, sin_f, caches, *, batch, steps):
    ng = len(SWA_GROUPS)
    assert steps & (steps - 1) == 0 and steps % SUBLANE == 0
    specs = []
    for (window, dil), c in zip(SWA_GROUPS, caches):
        assert c.shape[1] * steps == window and window // dil == SWA_SPAN, "cache must hold exactly one window"
        assert dil % steps == 0 or steps % dil == 0
        specs.append(pl.BlockSpec((1,) + c.shape[1:], lambda b: (b, 0, 0, 0)))
    any_spec = pl.BlockSpec(memory_space=pl.ANY)
    out = pl.pallas_call(
        functools.partial(_swa_decode_kernel, steps=steps),
        out_shape=(jax.ShapeDtypeStruct((ng * SWA_HPG, batch * steps, LANE), F32),)
        + tuple(jax.ShapeDtypeStruct(c.shape, F32) for c in caches),
        grid=(batch,),
        in_specs=[
            pl.BlockSpec((3 * ng * SWA_HPG, steps, LANE), lambda b: (0, b, 0)),
            pl.BlockSpec((steps, LANE), lambda b: (0, 0)),
            pl.BlockSpec((steps, LANE), lambda b: (0, 0)),
        ] + specs,
        out_specs=(pl.BlockSpec((ng * SWA_HPG, steps, LANE), lambda b: (0, b, 0)),) + (any_spec,) * ng,
        scratch_shapes=[pltpu.VMEM((steps * SWA_HPG * 2, LANE), F32)] * ng + [pltpu.SemaphoreType.DMA((ng, 2))],
        compiler_params=_params("arbitrary"),
        name="swa_decode",
    )(qkv_hm, cos_f, sin_f, *caches)
    return out[0], out[1:]


def _rope_tables(pos):
    half = LANE // 2
    inv = ROPE_THETA ** (-jnp.arange(half, dtype=F32) / half)
    ang = pos.astype(F32)[:, None] * inv[None, :]
    cos, sin = jnp.cos(ang), jnp.sin(ang)
    return jnp.concatenate([cos, cos], axis=1), jnp.concatenate([-sin, sin], axis=1)


def kernel(x_prompt, x_sample, state_gdn, state_gdn_conv, cache_swa_kv0, cache_swa_kv1, cache_swa_kv2,
           cache_mem_kv, state_ffn_conv, mem_prompt_in, norm_gains, gdn_w_in, gdn_conv_w, gdn_a_log,
           gdn_dt_bias, gdn_norm, gdn_w_out, swa_w_qkv, swa_w_out, mem_w_q, mem_w_kv, mem_w_o,
           ffn_w_up, ffn_conv_w, ffn_conv_b, ffn_w_down):
    depth = norm_gains.shape[0]
    d_model = x_prompt.shape[-1]
    qk_dim = GDN_QK_HEADS * LANE
    v_dim = GDN_V_HEADS * LANE
    conv_dim = 2 * qk_dim + v_dim
    main_dim = conv_dim + v_dim
    ncols = conv_dim // LANE

    gdn_w_out16, swa_w_out16, mem_w_o16, ffn_w_down16 = (
        w.astype(BF16) for w in (gdn_w_out, swa_w_out, mem_w_o, ffn_w_down))
    gdn_w_in_t = jnp.swapaxes(gdn_w_in, 1, 2).astype(BF16)
    swa_w_qkv16 = swa_w_qkv.astype(BF16)

    def gdn_weights(ia):
        w_tail = gdn_w_in[ia, :, main_dim:]
        w_gate = jnp.zeros((1, d_model, 2 * LANE), F32)
        w_gate = w_gate.at[0, :, :GDN_V_HEADS].set(w_tail[:, :GDN_V_HEADS])
        w_gate = w_gate.at[0, :, LANE:LANE + GDN_V_HEADS].set(w_tail[:, GDN_V_HEADS:])
        cw = gdn_conv_w[ia].reshape(GDN_CONV, ncols, LANE).transpose(1, 0, 2)
        cw = jnp.pad(cw, ((0, 0), (0, SUBLANE - GDN_CONV), (0, 0)))
        gp = jnp.zeros((SUBLANE, LANE), F32)
        gp = gp.at[0, :GDN_V_HEADS].set(gdn_a_log[ia]).at[1, :GDN_V_HEADS].set(gdn_dt_bias[ia])
        return w_gate, cw, gp, gdn_norm[ia]

    def run_trunk(x3, pos, prompt, mem_src):
        batch, seq_len, _ = x3.shape
        m = batch * seq_len
        x = x3.reshape(m, d_model)
        cos_f, sin_f = _rope_tables(pos)
        out = {"gdn_s": [], "gdn_c": [], "swa": [[] for _ in SWA_GROUPS], "ffn": [], "mem_kv": []}
        ia = ib = 0
        for layer in range(depth):
            g = norm_gains[layer]
            if layer % 2 == 0:
                w_gate, cw, gp, o_gain = gdn_weights(ia)
                p = norm_matmul(x, g[0], gdn_w_in_t, ia, n=main_dim, head_major=True, w_transposed=True)
                gates = norm_matmul(x, g[0], w_gate, 0)
                if prompt:
                    tail0 = jnp.zeros((batch, ncols, SUBLANE, LANE), F32)
                    s0 = jnp.zeros((batch, GDN_V_HEADS, LANE, LANE), F32)
                else:
                    cb = state_gdn_conv[ia].reshape(batch, GDN_CONV - 1, ncols, LANE).transpose(0, 2, 1, 3)
                    tail0 = jnp.pad(cb, ((0, 0), (0, 0), (SUBLANE - (GDN_CONV - 1), 0), (0, 0)))
                    s0 = state_gdn[ia]
                o_hm, s1 = gdn_core(p, gates, cw, gp, o_gain, tail0, s0, batch=batch, seq_len=seq_len)
                out["gdn_s"].append(s1)
                c1 = p.reshape(-1, batch, seq_len, LANE)[:ncols, :, seq_len - (GDN_CONV - 1):]
                out["gdn_c"].append(c1.transpose(1, 2, 0, 3).reshape(batch, GDN_CONV - 1, conv_dim))
                x = matmul_resnorm(o_hm, gdn_w_out16, ia, x, g[1], head_major=True)
                ia += 1
            else:
                ng = len(SWA_GROUPS)
                qkv_hm = norm_matmul(x, g[0], swa_w_qkv16, ib, head_major=True)
                if prompt:
                    o_hm, kr = swa_prompt(qkv_hm, cos_f, sin_f, batch=batch, seq_len=seq_len)
                    for gi, (window, _) in enumerate(SWA_GROUPS):
                        keep = min(window, seq_len)
                        kv = kv_pack(kr, qkv_hm, gi, batch=batch, seq_len=seq_len, keep=keep)
                        out["swa"][gi].append(kv.reshape(batch, keep, SWA_HPG, 2, LANE).transpose(0, 1, 3, 2, 4))
                else:
                    caches = [c[ib].transpose(0, 1, 3, 2, 4).reshape(batch, -1, seq_len * SWA_HPG * 2, LANE)
                              for c in (cache_swa_kv0, cache_swa_kv1, cache_swa_kv2)]
                    o_hm, rolled = swa_decode(qkv_hm, cos_f, sin_f, caches, batch=batch, steps=seq_len)
                    for gi in range(ng):
                        r5 = rolled[gi].reshape(batch, -1, SWA_HPG, 2, LANE)
                        out["swa"][gi].append(r5.transpose(0, 1, 3, 2, 4))
                x = matmul_resnorm(o_hm, swa_w_out16, ib, x, g[1], head_major=True)
                ib += 1
            q_hm = norm_matmul(x, g[2], mem_w_q, layer, head_major=True)
            if prompt:
                mem_len = mem_src.shape[1]
                kv_hm = norm_matmul(mem_src.reshape(batch * mem_len, d_model), g[2], mem_w_kv, layer,
                                    do_norm=False, head_major=True)
                out["mem_kv"].append(kv_hm.reshape(2, MEM_HEADS, batch, mem_len, LANE).transpose(2, 3, 0, 1, 4))
                o_hm = mem_prompt(q_hm, kv_hm, batch=batch, seq_len=seq_len, mem_len=mem_len, tq=1024)
            else:
                mem_rows = mem_src.reshape(mem_src.shape[0], batch, -1, LANE)
                o_hm = mem_decode(q_hm, mem_rows, layer, batch=batch, steps=seq_len)
            x = matmul_resnorm(o_hm, mem_w_o16, layer, x, g[3], head_major=True)
            if prompt:
                act, gt = ffn_up(x, g[4], ffn_w_up, layer, ffn_conv_w[layer], ffn_conv_b[layer], seq_len=seq_len)
                per_seq = gt.shape[0] // batch
                out["ffn"].append(gt[per_seq - 1::per_seq, SUBLANE - (FFN_CONV - 1):])
            else:
                buf = state_ffn_conv[layer]
                zero = jnp.zeros((batch, seq_len - 1, buf.shape[-1]), F32)
                hist1 = jnp.concatenate([buf[:, 1:2], zero], axis=1).reshape(m, -1)
                hist2 = jnp.concatenate([buf, zero[:, 1:]], axis=1).reshape(m, -1)
                act, gt = ffn_up(x, g[4], ffn_w_up, layer, ffn_conv_w[layer], ffn_conv_b[layer], seq_len=seq_len,
                                 hist=(hist1, hist2))
                out["ffn"].append(gt.reshape(batch, seq_len, -1)[:, seq_len - (FFN_CONV - 1):])
            x = matmul_resnorm(act, ffn_w_down16, layer, x, g[5], head_major=False)
        return x.reshape(batch, seq_len, d_model), out

    pos_p = jnp.arange(x_prompt.shape[1])
    y_p, op = run_trunk(x_prompt, pos_p, True, mem_prompt_in)
    pos_s = PAST_LEN + jnp.arange(x_sample.shape[1])
    y_s, os_ = run_trunk(x_sample, pos_s, False, cache_mem_kv)

    return (y_p, y_s, jnp.stack(op["gdn_s"]), jnp.stack(op["gdn_c"]),
            jnp.stack(op["swa"][0]), jnp.stack(op["swa"][1]), jnp.stack(op["swa"][2]),
            jnp.stack(op["mem_kv"]), jnp.stack(op["ffn"]),
            jnp.stack(os_["gdn_s"]), jnp.stack(os_["gdn_c"]),
            jnp.stack(os_["swa"][0]), jnp.stack(os_["swa"][1]), jnp.stack(os_["swa"][2]),
            jnp.stack(os_["ffn"]))
```

```python
import functools
import math

import jax
import jax.numpy as jnp
from jax import lax
from jax.experimental import pallas as pl
from jax.experimental.pallas import tpu as pltpu

F32 = jnp.float32
BF16 = jnp.bfloat16
HIGHEST = lax.Precision.HIGHEST

LANE = 128
SUBLANE = 8
NORM_EPS = 1e-6
ROPE_THETA = 10000.0
PAST_LEN = 2048
GDN_CHUNK = 64
GDN_QK_HEADS = 16
GDN_V_HEADS = 32
GDN_CONV = 4
SWA_GROUPS = ((128, 1), (512, 4), (2048, 16))
SWA_HPG = 6
SWA_SPAN = 128
MEM_HEADS = 4
FFN_CONV = 3
VMEM_LIMIT_BYTES = 52 * 1024 * 1024
MRN_COL_CHUNK = 512


ROW_TILE = 1024
STREAM_BLOCK_BYTES = int(4.75 * 1024 * 1024)
PAIR_BLOCK_BYTES = int(8.5 * 1024 * 1024)


def _row_tile(m):
    return ROW_TILE if m % ROW_TILE == 0 else m


def _lane_tile(dim, bytes_per_unit, budget):
    fits = [t for t in range(LANE, dim + 1, LANE) if dim % t == 0 and t * bytes_per_unit <= budget]
    return max(fits) if fits else LANE


def _params(*semantics):
    return pltpu.CompilerParams(dimension_semantics=semantics, vmem_limit_bytes=VMEM_LIMIT_BYTES)


def _dot(a, b):
    return jnp.dot(a, b, preferred_element_type=F32)


def _dot_nt(a, b):
    return lax.dot_general(a, b, (((1,), (1,)), ((), ())), preferred_element_type=F32)


def _dot_tn(a, b):
    return lax.dot_general(a, b, (((0,), (0,)), ((), ())), preferred_element_type=F32)


def _rms(x, gain):
    return x * lax.rsqrt(jnp.mean(x * x, axis=-1, keepdims=True) + NORM_EPS) * gain


def _silu(x):
    return x * jax.nn.sigmoid(x)


def _nmm_kernel(x_ref, g_ref, w_ref, o_ref, h_ref, *, do_norm, head_major, w_transposed):
    @pl.when(pl.program_id(1) == 0)
    def _():
        x = x_ref[...]
        if do_norm:
            x = _rms(x, g_ref[...])
        h_ref[...] = x.astype(BF16)

    if w_transposed:
        acc = _dot_nt(h_ref[...], w_ref[0].astype(BF16))
    else:
        acc = _dot(h_ref[...], w_ref[0].astype(BF16))
    if head_major:
        for c in range(o_ref.shape[0]):
            o_ref[c] = acc[:, c * LANE:(c + 1) * LANE]
    else:
        o_ref[...] = acc


def norm_matmul(x, gain, w, layer, *, n=None, do_norm=True, head_major=False, w_transposed=False):
    m, k = x.shape
    n = w.shape[1 if w_transposed else 2] if n is None else n
    tm = _row_tile(m)
    tn = _lane_tile(n, k * w.dtype.itemsize, STREAM_BLOCK_BYTES)
    assert m % tm == 0 and n % tn == 0 and tn % LANE == 0
    if w_transposed:
        w_spec = pl.BlockSpec((1, tn, k), lambda i, j: (layer, j, 0))
    else:
        w_spec = pl.BlockSpec((1, k, tn), lambda i, j: (layer, 0, j))
    if head_major:
        out_shape = jax.ShapeDtypeStruct((n // LANE, m, LANE), F32)
        out_spec = pl.BlockSpec((tn // LANE, tm, LANE), lambda i, j: (j, i, 0))
    else:
        out_shape = jax.ShapeDtypeStruct((m, n), F32)
        out_spec = pl.BlockSpec((tm, tn), lambda i, j: (i, j))
    return pl.pallas_call(
        functools.partial(_nmm_kernel, do_norm=do_norm, head_major=head_major, w_transposed=w_transposed),
        out_shape=out_shape,
        grid=(m // tm, n // tn),
        in_specs=[
            pl.BlockSpec((tm, k), lambda i, j: (i, 0)),
            pl.BlockSpec((1, k), lambda i, j: (0, 0)),
            w_spec,
        ],
        out_specs=out_spec,
        scratch_shapes=[pltpu.VMEM((tm, k), BF16)],
        compiler_params=_params("parallel", "arbitrary"),
        name="norm_matmul",
    )(x, gain.reshape(1, k), w)


def _mrn_kernel(a_ref, w_ref, x_ref, g_ref, o_ref, *, head_major):
    k = pl.program_id(1)
    if head_major:
        a = jnp.concatenate([a_ref[c].astype(BF16) for c in range(a_ref.shape[0])], axis=1)
    else:
        a = a_ref[...].astype(BF16)
    d = o_ref.shape[1]
    cols = [slice(c, c + MRN_COL_CHUNK) for c in range(0, d, MRN_COL_CHUNK)]
    @pl.when(k == 0)
    def _():
        o_ref[...] = jnp.zeros_like(o_ref)

    for cs in cols:
        o_ref[:, cs] += _dot(a, w_ref[0, :, cs])

    @pl.when(k == pl.num_programs(1) - 1)
    def _():
        ss = functools.reduce(lambda x, y: x + y,
                              [jnp.sum(o_ref[:, cs] * o_ref[:, cs], axis=-1, keepdims=True) for cs in cols])
        inv = lax.rsqrt(ss * (1.0 / d) + NORM_EPS)
        for cs in cols:
            o_ref[:, cs] = x_ref[:, cs] + (o_ref[:, cs] * inv) * g_ref[:, cs]


def matmul_resnorm(a, w, layer, x, gain, *, head_major):
    m, d = x.shape
    kdim = w.shape[1]
    tm = _row_tile(m)
    tk = _lane_tile(kdim, d * w.dtype.itemsize + tm * a.dtype.itemsize, PAIR_BLOCK_BYTES)
    assert m % tm == 0 and kdim % tk == 0 and tk % LANE == 0
    if head_major:
        a_spec = pl.BlockSpec((tk // LANE, tm, LANE), lambda i, k: (k, i, 0))
    else:
        a_spec = pl.BlockSpec((tm, tk), lambda i, k: (i, k))
    return pl.pallas_call(
        functools.partial(_mrn_kernel, head_major=head_major),
        out_shape=jax.ShapeDtypeStruct((m, d), F32),
        grid=(m // tm, kdim // tk),
        in_specs=[
            a_spec,
            pl.BlockSpec((1, tk, d), lambda i, k: (layer, k, 0)),
            pl.BlockSpec((tm, d), lambda i, k: (i, 0)),
            pl.BlockSpec((1, d), lambda i, k: (0, 0)),
        ],
        out_specs=pl.BlockSpec((tm, d), lambda i, k: (i, 0)),
        compiler_params=_params("parallel", "arbitrary"),
        name="matmul_resnorm",
    )(a, w, x, gain.reshape(1, d))


def _gelu_tanh(x):
    c = math.sqrt(2.0 / math.pi)
    return x * (0.5 * (1.0 + jnp.tanh(c * (x + 0.044715 * (x * x * x)))))


def _ffn_up_kernel(*refs, seq_len, tm, long_seq):
    if long_seq:
        x_ref, g_ref, wg_ref, wu_ref, cw_ref, cb_ref, act_ref, gt_ref, h_ref, carry_ref = refs
    else:
        x_ref, g_ref, wg_ref, wu_ref, cw_ref, cb_ref, h1_ref, h2_ref, act_ref, gt_ref, h_ref = refs
    i = pl.program_id(0)
    j = pl.program_id(1)

    @pl.when(j == 0)
    def _():
        h_ref[...] = _rms(x_ref[...], g_ref[...]).astype(BF16)

    h = h_ref[...]
    gate = _dot(h, wg_ref[0].astype(BF16))
    up = _dot(h, wu_ref[0].astype(BF16))
    row = lax.broadcasted_iota(jnp.int32, gate.shape, 0)
    sh1 = pltpu.roll(gate, 1, axis=0)
    sh2 = pltpu.roll(gate, 2, axis=0)
    if long_seq:
        prev = jnp.where((i * tm) % seq_len == 0, 0.0, carry_ref[j])
        carry_ref[j] = gate[tm - SUBLANE:tm]
        gt_ref[0] = gate[tm - SUBLANE:tm]
        sh1 = jnp.where(row == 0, prev[7:8], sh1)
        sh2 = jnp.where(row == 0, prev[6:7], jnp.where(row == 1, prev[7:8], sh2))
    else:
        pos = row % SUBLANE
        gt_ref[...] = gate
        sh1 = jnp.where(pos == 0, 0.0, sh1) + h1_ref[...]
        sh2 = jnp.where(pos < 2, 0.0, sh2) + h2_ref[...]
    y = sh2 * cw_ref[0:1] + sh1 * cw_ref[1:2] + gate * cw_ref[2:3]
    act_ref[...] = (_gelu_tanh(y + cb_ref[...]) * up).astype(BF16)


def ffn_up(x, gain, w_up, layer, conv_w, conv_b, *, seq_len, hist=None):
    m, k = x.shape
    f = conv_w.shape[1]
    tm = _row_tile(m)
    tn = _lane_tile(f, 2 * k * w_up.dtype.itemsize, PAIR_BLOCK_BYTES)
    nj = f // tn
    assert m % tm == 0 and f % tn == 0
    long_seq = seq_len >= tm
    if long_seq:
        assert seq_len % tm == 0
    else:
        assert seq_len == SUBLANE
    in_specs = [
        pl.BlockSpec((tm, k), lambda i, j: (i, 0)),
        pl.BlockSpec((1, k), lambda i, j: (0, 0)),
        pl.BlockSpec((1, k, tn), lambda i, j: (layer, 0, j)),
        pl.BlockSpec((1, k, tn), lambda i, j: (layer, 0, j + nj)),
        pl.BlockSpec((FFN_CONV, tn), lambda i, j: (0, j)),
        pl.BlockSpec((1, tn), lambda i, j: (0, j)),
    ]
    args = [x, gain.reshape(1, k), w_up, w_up, conv_w, conv_b.reshape(1, f)]
    scratch = [pltpu.VMEM((tm, k), BF16)]
    if long_seq:
        gt_shape = jax.ShapeDtypeStruct((m // tm, SUBLANE, f), F32)
        gt_spec = pl.BlockSpec((1, SUBLANE, tn), lambda i, j: (i, 0, j))
        scratch.append(pltpu.VMEM((nj, SUBLANE, tn), F32))
    else:
        gt_shape = jax.ShapeDtypeStruct((m, f), F32)
        gt_spec = pl.BlockSpec((tm, tn), lambda i, j: (i, j))
        in_specs += [pl.BlockSpec((tm, tn), lambda i, j: (i, j))] * 2
        args += list(hist)
    return pl.pallas_call(
        functools.partial(_ffn_up_kernel, seq_len=seq_len, tm=tm, long_seq=long_seq),
        out_shape=(jax.ShapeDtypeStruct((m, f), BF16), gt_shape),
        grid=(m // tm, nj),
        in_specs=in_specs,
        out_specs=(pl.BlockSpec((tm, tn), lambda i, j: (i, j)), gt_spec),
        scratch_shapes=scratch,
        compiler_params=_params("arbitrary", "arbitrary"),
        name="ffn_up",
    )(*args)


def _gdn_kernel(p_ref, g_ref, cw_ref, gp_ref, gain_ref, tail0_ref, s0_ref, o_ref, s_ref,
                tail_ref, gct_ref, *, chunk, heads_per_group):
    c = chunk
    nqk = GDN_QK_HEADS
    rep = GDN_V_HEADS // GDN_QK_HEADS
    stack_state_rows = 2 * c <= 2 * SUBLANE

    @pl.when(pl.program_id(1) == 0)
    def _():
        s_ref[...] = s0_ref[...]
        tail_ref[:, 0:SUBLANE] = tail0_ref[0]

    gates = g_ref[...]
    a_log = gp_ref[0:1, :]
    dt_bias = gp_ref[1:2, :]
    beta_all = jax.nn.sigmoid(gates[:, :LANE])
    xa = gates[:, LANE:] + dt_bias
    softplus = jnp.maximum(xa, 0.0) + jnp.log1p(jnp.exp(-jnp.abs(xa)))
    g_all = -jnp.exp(a_log) * softplus
    row = lax.broadcasted_iota(jnp.int32, (c, c), 0)
    col = lax.broadcasted_iota(jnp.int32, (c, c), 1)
    causal = row >= col
    strict = row > col
    eye = (row == col).astype(F32)
    gc_all = jnp.dot(causal.astype(F32), g_all, precision=HIGHEST, preferred_element_type=F32)
    gct_ref[...] = gc_all.T
    gain = gain_ref[...]

    def conv_silu(cb):
        x = p_ref[cb]
        w = cw_ref[cb]
        tail_ref[cb, SUBLANE:SUBLANE + c] = x
        y = (tail_ref[cb, pl.ds(5, c)] * w[0:1] + tail_ref[cb, pl.ds(6, c)] * w[1:2]
             + tail_ref[cb, pl.ds(7, c)] * w[2:3] + x * w[3:4])
        tail_ref[cb, 0:SUBLANE] = x[c - SUBLANE:c]
        return _silu(y)

    def l2norm(x):
        return x * lax.rsqrt(jnp.sum(x * x, axis=-1, keepdims=True) + NORM_EPS)

    def head_group(qk_heads):
        v_heads = [rep * j + e for j in qk_heads for e in range(rep)]
        q, k, kk, qk = {}, {}, {}, {}
        for j in qk_heads:
            q[j] = l2norm(conv_silu(j)) * (LANE ** -0.5)
            k[j] = l2norm(conv_silu(nqk + j))
        for j in qk_heads:
            k16 = k[j].astype(BF16)
            kk[j] = _dot_nt(k16, k16)
            qk[j] = _dot_nt(q[j].astype(BF16), k16)
        beta_c, gc_c, a_intra, xm, tinv = {}, {}, {}, {}, {}
        for hv in v_heads:
            j = hv // rep
            beta_c[hv] = beta_all[:, hv:hv + 1]
            gc_c[hv] = gc_all[:, hv:hv + 1]
            diff = gc_c[hv] - gct_ref[hv:hv + 1, :]
            decay = jnp.where(causal, jnp.exp(jnp.where(causal, diff, 0.0)), 0.0)
            a_intra[hv] = (qk[j] * decay).astype(BF16)
            xm[hv] = -jnp.where(strict, (beta_c[hv] * kk[j]) * decay, 0.0)
            tinv[hv] = eye + xm[hv]
        for _ in range(int(math.log2(c)) - 1):
            for hv in v_heads:
                xm16 = xm[hv].astype(BF16)
                xm[hv] = _dot(xm16, xm16)
            for hv in v_heads:
                tinv[hv] = tinv[hv] + _dot(tinv[hv].astype(BF16), xm[hv].astype(BF16))
        uw = {}
        for hv in v_heads:
            j = hv // rep
            v = conv_silu(2 * nqk + hv)
            tinv16 = tinv[hv].astype(BF16)
            vb16 = (v * beta_c[hv]).astype(BF16)
            kbg16 = (k[j] * (beta_c[hv] * jnp.exp(gc_c[hv]))).astype(BF16)
            if stack_state_rows:
                uw[hv] = _dot(tinv16, jnp.concatenate([vb16, kbg16], axis=1))
            else:
                uw[hv] = jnp.concatenate([_dot(tinv16, vb16), _dot(tinv16, kbg16)], axis=1)
        v_new16, o_inter = {}, {}
        for hv in v_heads:
            j = hv // rep
            s16 = s_ref[0, hv].astype(BF16)
            w16 = uw[hv][:, LANE:].astype(BF16)
            qg16 = (q[j] * jnp.exp(gc_c[hv])).astype(BF16)
            if stack_state_rows:
                ws_qs = _dot(jnp.concatenate([w16, qg16], axis=0), s16)
                ws, o_inter[hv] = ws_qs[:c], ws_qs[c:]
            else:
                ws, o_inter[hv] = _dot(w16, s16), _dot(qg16, s16)
            v_new16[hv] = (uw[hv][:, :LANE] - ws).astype(BF16)
        for hv in v_heads:
            j = hv // rep
            gl = gc_c[hv][c - 1:c, :]
            o = o_inter[hv] + _dot(a_intra[hv], v_new16[hv])
            k_dec = k[j] * jnp.exp(gl - gc_c[hv])
            s_ref[0, hv] = s_ref[0, hv] * jnp.exp(gl) + _dot_tn(k_dec.astype(BF16), v_new16[hv])
            z = p_ref[2 * nqk + GDN_V_HEADS + hv]
            o_ref[hv] = (_rms(o, gain) * _silu(z)).astype(o_ref.dtype)

    for j0 in range(0, nqk, heads_per_group):
        head_group(range(j0, j0 + heads_per_group))


def gdn_core(p, gates, conv_w_hm, gate_params, o_gain, tail0, state0, *, batch, seq_len, heads_per_group=GDN_QK_HEADS):
    nblk, m, _ = p.shape
    c = min(GDN_CHUNK, seq_len)
    assert seq_len % c == 0 and c % SUBLANE == 0 and m == batch * seq_len
    nc = seq_len // c
    ncols = 2 * GDN_QK_HEADS + GDN_V_HEADS
    o_dtype = BF16 if c % (2 * SUBLANE) == 0 else F32
    return pl.pallas_call(
        functools.partial(_gdn_kernel, chunk=c, heads_per_group=heads_per_group),
        out_shape=(jax.ShapeDtypeStruct((GDN_V_HEADS, m, LANE), o_dtype),
                   jax.ShapeDtypeStruct((batch, GDN_V_HEADS, LANE, LANE), F32)),
        grid=(batch, nc),
        in_specs=[
            pl.BlockSpec((nblk, c, LANE), lambda b, n: (0, b * nc + n, 0)),
            pl.BlockSpec((c, 2 * LANE), lambda b, n: (b * nc + n, 0)),
            pl.BlockSpec((ncols, SUBLANE, LANE), lambda b, n: (0, 0, 0)),
            pl.BlockSpec((SUBLANE, LANE), lambda b, n: (0, 0)),
            pl.BlockSpec((1, LANE), lambda b, n: (0, 0)),
            pl.BlockSpec((1, ncols, SUBLANE, LANE), lambda b, n: (b, 0, 0, 0)),
            pl.BlockSpec((1, GDN_V_HEADS, LANE, LANE), lambda b, n: (b, 0, 0, 0)),
        ],
        out_specs=(pl.BlockSpec((GDN_V_HEADS, c, LANE), lambda b, n: (0, b * nc + n, 0)),
                   pl.BlockSpec((1, GDN_V_HEADS, LANE, LANE), lambda b, n: (b, 0, 0, 0))),
        scratch_shapes=[pltpu.VMEM((ncols, SUBLANE + c, LANE), F32), pltpu.VMEM((LANE, c), F32)],
        compiler_params=_params("parallel", "arbitrary"),
        name="gdn_core",
    )(p, gates, conv_w_hm, gate_params, o_gain.reshape(1, LANE), tail0, state0)


def _rope(x, cos_f, sin_f):
    return x * cos_f + pltpu.roll(x, LANE // 2, axis=1) * sin_f


def _swa_prompt_kernel(qkv_ref, cos_ref, sin_ref, o_ref, kr_ref, qs_ref, lse_ref, *, seq_len):
    g = pl.program_id(2)
    cos_f = cos_ref[...]
    sin_f = sin_ref[...]
    qs_ref[...] = _rope(qkv_ref[0, 0, 0], cos_f, sin_f)
    kr_ref[0, 0] = _rope(qkv_ref[1, 0, 0], cos_f, sin_f)
    blk = SWA_SPAN
    ii = lax.broadcasted_iota(jnp.int32, (blk, 2 * blk), 0)
    jj = lax.broadcasted_iota(jnp.int32, (blk, 2 * blk), 1)
    dist = ii + blk - jj
    band = (dist >= 0) & (dist <= SWA_SPAN)
    scale = LANE ** -0.5
    units_per_iter = 8
    assert (seq_len // blk) % units_per_iter == 0

    for gi, (window, dil) in enumerate(SWA_GROUPS):
        assert window // dil == SWA_SPAN and seq_len % (blk * dil) == 0
        nblk = seq_len // dil // blk

        @pl.when(g == gi)
        def _(gi=gi, dil=dil, nblk=nblk):
            def units(it, carry):
                cur, first_key, q, kcat, vcat = [], [], [], [], []
                for i in range(units_per_iter):
                    u = it * units_per_iter + i
                    r = u // nblk
                    b = u % nblk
                    cur.append(pl.ds(r + b * blk * dil, blk, stride=dil))
                    prv = pl.ds(r + jnp.maximum(b - 1, 0) * blk * dil, blk, stride=dil)
                    first_key.append(jnp.where(b > 0, 0, blk))
                    q.append(qs_ref[cur[i], :].astype(BF16))
                    kcat.append(jnp.concatenate([kr_ref[0, 0, prv, :], kr_ref[0, 0, cur[i], :]], axis=0).astype(BF16))
                    vcat.append(jnp.concatenate([qkv_ref[2, 0, 0, prv, :], qkv_ref[2, 0, 0, cur[i], :]],
                                                axis=0).astype(BF16))
                s = [_dot_nt(q[i], kcat[i]) * scale for i in range(units_per_iter)]
                p = []
                for i in range(units_per_iter):
                    si = jnp.where(band & (jj >= first_key[i]), s[i], -jnp.inf)
                    mx = jnp.max(si, axis=-1, keepdims=True)
                    e = jnp.exp(si - mx)
                    den = jnp.sum(e, axis=-1, keepdims=True)
                    p.append((e * (1.0 / den)).astype(BF16))
                    lse_ref[gi, cur[i], :] = jnp.broadcast_to(mx + jnp.log(den), (blk, LANE))
                for i in range(units_per_iter):
                    o_ref[gi, 0, cur[i], :] = _dot(p[i], vcat[i])
                return carry

            lax.fori_loop(0, seq_len // blk // units_per_iter, units, 0)

    @pl.when(g == len(SWA_GROUPS) - 1)
    def _():
        lses = [lse_ref[gi] for gi in range(len(SWA_GROUPS))]
        mx = functools.reduce(jnp.maximum, lses)
        es = [jnp.exp(l - mx) for l in lses]
        den = functools.reduce(lambda a, b: a + b, es)
        for gi in range(len(SWA_GROUPS)):
            o_ref[gi, 0] = o_ref[gi, 0] * (es[gi] / den)


def swa_prompt(qkv_hm, cos_f, sin_f, *, batch, seq_len):
    ng = len(SWA_GROUPS)
    m = batch * seq_len
    qkv5 = qkv_hm.reshape(3, ng, SWA_HPG, m, LANE)
    o, kr = pl.pallas_call(
        functools.partial(_swa_prompt_kernel, seq_len=seq_len),
        out_shape=(jax.ShapeDtypeStruct((ng, SWA_HPG, m, LANE), F32),
                   jax.ShapeDtypeStruct((ng, SWA_HPG, m, LANE), F32)),
        grid=(batch, SWA_HPG, ng),
        in_specs=[
            pl.BlockSpec((3, 1, 1, seq_len, LANE), lambda b, h, g: (0, g, h, b, 0)),
            pl.BlockSpec((seq_len, LANE), lambda b, h, g: (0, 0)),
            pl.BlockSpec((seq_len, LANE), lambda b, h, g: (0, 0)),
        ],
        out_specs=(pl.BlockSpec((ng, 1, seq_len, LANE), lambda b, h, g: (0, h, b, 0)),
                   pl.BlockSpec((1, 1, seq_len, LANE), lambda b, h, g: (g, h, b, 0))),
        scratch_shapes=[pltpu.VMEM((seq_len, LANE), F32), pltpu.VMEM((ng, seq_len, LANE), F32)],
        compiler_params=_params("parallel", "parallel", "arbitrary"),
        name="swa_prompt",
    )(qkv5, cos_f, sin_f)
    return o.reshape(ng * SWA_HPG, m, LANE), kr


def _kv_pack_kernel(k_ref, v_ref, o_ref):
    nh, rows, _ = k_ref.shape[1:]
    for h in range(nh):
        o_ref[0, pl.ds(2 * h, rows, stride=2 * nh), :] = k_ref[0, h]
        o_ref[0, pl.ds(2 * h + 1, rows, stride=2 * nh), :] = v_ref[h]


def kv_pack(k_rot, qkv_hm, group, *, batch, seq_len, keep):
    ng, nh = k_rot.shape[:2]
    rows = min(keep, 512)
    assert keep % rows == 0 and (seq_len - keep) % rows == 0
    first = (seq_len - keep) // rows
    per_seq = seq_len // rows
    return pl.pallas_call(
        _kv_pack_kernel,
        out_shape=jax.ShapeDtypeStruct((batch, keep * nh * 2, LANE), F32),
        grid=(batch, keep // rows),
        in_specs=[
            pl.BlockSpec((1, nh, rows, LANE), lambda b, c: (group, 0, b * per_seq + first + c, 0)),
            pl.BlockSpec((nh, rows, LANE), lambda b, c: (2 * ng + group, b * per_seq + first + c, 0)),
        ],
        out_specs=pl.BlockSpec((1, rows * nh * 2, LANE), lambda b, c: (b, c, 0)),
        compiler_params=_params("parallel", "parallel"),
        name="kv_pack",
    )(k_rot, qkv_hm)


def _mem_prompt_kernel(q_ref, k_ref, v_ref, o_ref):
    s = _dot_nt(q_ref[0].astype(BF16), k_ref[0].astype(BF16)) * (LANE ** -0.5)
    mx = jnp.max(s, axis=-1, keepdims=True)
    e = jnp.exp(s - mx)
    p = e / jnp.sum(e, axis=-1, keepdims=True)
    o_ref[0] = _dot(p.astype(BF16), v_ref[0].astype(BF16))


def mem_prompt(q_hm, kv_hm, *, batch, seq_len, mem_len, tq):
    m = batch * seq_len
    nq = seq_len // tq
    return pl.pallas_call(
        _mem_prompt_kernel,
        out_shape=jax.ShapeDtypeStruct((MEM_HEADS, m, LANE), F32),
        grid=(batch, MEM_HEADS, nq),
        in_specs=[
            pl.BlockSpec((1, tq, LANE), lambda b, h, i: (h, b * nq + i, 0)),
            pl.BlockSpec((1, mem_len, LANE), lambda b, h, i: (h, b, 0)),
            pl.BlockSpec((1, mem_len, LANE), lambda b, h, i: (MEM_HEADS + h, b, 0)),
        ],
        out_specs=pl.BlockSpec((1, tq, LANE), lambda b, h, i: (h, b * nq + i, 0)),
        compiler_params=_params("parallel", "parallel", "parallel"),
        name="mem_prompt",
    )(q_hm, kv_hm, kv_hm)


def _mem_decode_kernel(q_ref, c_ref, o_ref, *, steps):
    nh = q_ref.shape[0]
    nb = c_ref.shape[1]
    mem_len = c_ref.shape[2] // (2 * nh)
    pairs = [(i, h) for i in range(nb) for h in range(nh)]
    s, v = {}, {}
    for i, h in pairs:
        k = c_ref[0, i, pl.ds(h, mem_len, stride=2 * nh), :].astype(BF16)
        v[i, h] = c_ref[0, i, pl.ds(nh + h, mem_len, stride=2 * nh), :].astype(BF16)
        q = q_ref[h, i * steps:(i + 1) * steps, :].astype(BF16)
        s[i, h] = _dot_nt(q, k) * (LANE ** -0.5)
    p = {}
    for key in pairs:
        e = jnp.exp(s[key] - jnp.max(s[key], axis=-1, keepdims=True))
        p[key] = (e / jnp.sum(e, axis=-1, keepdims=True)).astype(BF16)
    for i, h in pairs:
        o_ref[h, i * steps:(i + 1) * steps, :] = _dot(p[i, h], v[i, h])


def mem_decode(q_hm, cache, layer, *, batch, steps, batch_per_step=4):
    h = q_hm.shape[0]
    nb = batch_per_step if batch % batch_per_step == 0 else 1
    return pl.pallas_call(
        functools.partial(_mem_decode_kernel, steps=steps),
        out_shape=jax.ShapeDtypeStruct(q_hm.shape, F32),
        grid=(batch // nb,),
        in_specs=[
            pl.BlockSpec((h, nb * steps, LANE), lambda b: (0, b, 0)),
            pl.BlockSpec((1, nb, cache.shape[2], LANE), lambda b: (layer, b, 0, 0)),
        ],
        out_specs=pl.BlockSpec((h, nb * steps, LANE), lambda b: (0, b, 0)),
        compiler_params=_params("parallel"),
        name="mem_decode",
    )(q_hm, cache)


def _swa_decode_kernel(qkv_ref, cos_ref, sin_ref, c0_ref, c1_ref, c2_ref,
                       o_ref, r0_ref, r1_ref, r2_ref, n0_ref, n1_ref, n2_ref, sem, *, steps):
    b = pl.program_id(0)
    ng = len(SWA_GROUPS)
    caches = (c0_ref, c1_ref, c2_ref)
    rolled = (r0_ref, r1_ref, r2_ref)
    staged = (n0_ref, n1_ref, n2_ref)
    rows_per_t = 2 * SWA_HPG
    scale = LANE ** -0.5

    def roll_copy(g):
        keep = caches[g].shape[1] - 1
        return pltpu.make_async_copy(caches[g].at[0, pl.ds(1, keep)], rolled[g].at[b, pl.ds(0, keep)], sem.at[g, 0])

    def new_copy(g):
        return pltpu.make_async_copy(staged[g], rolled[g].at[b, caches[g].shape[1] - 1], sem.at[g, 1])

    for g in range(ng):
        roll_copy(g).start()

    cos_f = cos_ref[...]
    sin_f = sin_ref[...]
    u_col = lax.broadcasted_iota(jnp.int32, (steps, 1), 0)
    outs, lses = {}, {}
    for g, (window, dil) in enumerate(SWA_GROUPS):
        chunk_stride = max(dil // steps, 1)
        n_chunks = caches[g].shape[1] // chunk_stride
        shifts = list(range(0, steps, dil)) if dil < steps else [0]
        c_idx = lax.broadcasted_iota(jnp.int32, (n_chunks, steps, 1), 0)
        u_idx = lax.broadcasted_iota(jnp.int32, (n_chunks, steps, 1), 1)
        tau = c_idx * (steps * chunk_stride) + u_idx
        for h in range(SWA_HPG):
            hd = g * SWA_HPG + h
            q = _rope(qkv_ref[hd], cos_f, sin_f)
            k_new = _rope(qkv_ref[ng * SWA_HPG + hd], cos_f, sin_f)
            v_new = qkv_ref[2 * ng * SWA_HPG + hd]
            staged[g][pl.ds(2 * h, steps, stride=rows_per_t), :] = k_new
            staged[g][pl.ds(2 * h + 1, steps, stride=rows_per_t), :] = v_new
            k_c = caches[g][0, pl.ds(0, n_chunks, stride=chunk_stride), pl.ds(2 * h, steps, stride=rows_per_t), :]
            v_c = caches[g][0, pl.ds(0, n_chunks, stride=chunk_stride), pl.ds(2 * h + 1, steps, stride=rows_per_t), :]
            parts = []
            for a in shifts:
                q_a = q if a == 0 else pltpu.roll(q, steps - a, axis=0)
                sc = jnp.sum(k_c * q_a[None], axis=-1, keepdims=True) * scale
                sn = jnp.sum(k_new * q_a, axis=-1, keepdims=True) * scale
                if a > 0:
                    s_idx = (u_idx + a) & (steps - 1)
                    sc = jnp.where(tau >= s_idx, sc, -jnp.inf)
                    sn = jnp.where(u_col + a < steps, sn, -jnp.inf)
                mx = jnp.maximum(jnp.max(sc, axis=0), sn)
                e = jnp.exp(sc - mx[None])
                en = jnp.exp(sn - mx)
                den = jnp.sum(e, axis=0) + en
                acc = jnp.sum(e * v_c, axis=0) + en * v_new
                mx = jnp.broadcast_to(mx, (steps, LANE))
                den = jnp.broadcast_to(den, (steps, LANE))
                if a > 0:
                    mx, den, acc = (pltpu.roll(x, a, axis=0) for x in (mx, den, acc))
                parts.append((mx, den, acc))
            mx = functools.reduce(jnp.maximum, [p[0] for p in parts])
            ws = [jnp.exp(p[0] - mx) for p in parts]
            den = functools.reduce(lambda x, y: x + y, [w * p[1] for w, p in zip(ws, parts)])
            acc = functools.reduce(lambda x, y: x + y, [w * p[2] for w, p in zip(ws, parts)])
            outs[g, h] = acc / den
            lses[g, h] = mx + jnp.log(den)

    for g in range(ng):
        new_copy(g).start()

    for h in range(SWA_HPG):
        mx = functools.reduce(jnp.maximum, [lses[g, h] for g in range(ng)])
        es = [jnp.exp(lses[g, h] - mx) for g in range(ng)]
        den = functools.reduce(lambda x, y: x + y, es)
        for g in range(ng):
            o_ref[g * SWA_HPG + h] = outs[g, h] * (es[g] / den)

    for g in range(ng):
        roll_copy(g).wait()
        new_copy(g).wait()


def swa_decode(qkv_hm, cos_f, sin_f, caches, *, batch, steps):
    ng = len(SWA_GROUPS)
    assert steps & (steps - 1) == 0 and steps % SUBLANE == 0
    specs = []
    for (window, dil), c in zip(SWA_GROUPS, caches):
        assert c.shape[1] * steps == window and window // dil == SWA_SPAN, "cache must hold exactly one window"
        assert dil % steps == 0 or steps % dil == 0
        specs.append(pl.BlockSpec((1,) + c.shape[1:], lambda b: (b, 0, 0, 0)))
    any_spec = pl.BlockSpec(memory_space=pl.ANY)
    out = pl.pallas_call(
        functools.partial(_swa_decode_kernel, steps=steps),
        out_shape=(jax.ShapeDtypeStruct((ng * SWA_HPG, batch * steps, LANE), F32),)
        + tuple(jax.ShapeDtypeStruct(c.shape, F32) for c in caches),
        grid=(batch,),
        in_specs=[
            pl.BlockSpec((3 * ng * SWA_HPG, steps, LANE), lambda b: (0, b, 0)),
            pl.BlockSpec((steps, LANE), lambda b: (0, 0)),
            pl.BlockSpec((steps, LANE), lambda b: (0, 0)),
        ] + specs,
        out_specs=(pl.BlockSpec((ng * SWA_HPG, steps, LANE), lambda b: (0, b, 0)),) + (any_spec,) * ng,
        scratch_shapes=[pltpu.VMEM((steps * SWA_HPG * 2, LANE), F32)] * ng + [pltpu.SemaphoreType.DMA((ng, 2))],
        compiler_params=_params("arbitrary"),
        name="swa_decode",
    )(qkv_hm, cos_f, sin_f, *caches)
    return out[0], out[1:]


def _rope_tables(pos):
    half = LANE // 2
    inv = ROPE_THETA ** (-jnp.arange(half, dtype=F32) / half)
    ang = pos.astype(F32)[:, None] * inv[None, :]
    cos, sin = jnp.cos(ang), jnp.sin(ang)
    return jnp.concatenate([cos, cos], axis=1), jnp.concatenate([-sin, sin], axis=1)


def kernel(x_prompt, x_sample, state_gdn, state_gdn_conv, cache_swa_kv0, cache_swa_kv1, cache_swa_kv2,
           cache_mem_kv, state_ffn_conv, mem_prompt_in, norm_gains, gdn_w_in, gdn_conv_w, gdn_a_log,
           gdn_dt_bias, gdn_norm, gdn_w_out, swa_w_qkv, swa_w_out, mem_w_q, mem_w_kv, mem_w_o,
           ffn_w_up, ffn_conv_w, ffn_conv_b, ffn_w_down):
    depth = norm_gains.shape[0]
    d_model = x_prompt.shape[-1]
    qk_dim = GDN_QK_HEADS * LANE
    v_dim = GDN_V_HEADS * LANE
    conv_dim = 2 * qk_dim + v_dim
    main_dim = conv_dim + v_dim
    ncols = conv_dim // LANE

    gdn_w_out16, swa_w_out16, mem_w_o16, ffn_w_down16 = (
        w.astype(BF16) for w in (gdn_w_out, swa_w_out, mem_w_o, ffn_w_down))
    gdn_w_in_t = jnp.swapaxes(gdn_w_in, 1, 2).astype(BF16)
    swa_w_qkv16 = swa_w_qkv.astype(BF16)

    def gdn_weights(ia):
        w_tail = gdn_w_in[ia, :, main_dim:]
        w_gate = jnp.zeros((1, d_model, 2 * LANE), F32)
        w_gate = w_gate.at[0, :, :GDN_V_HEADS].set(w_tail[:, :GDN_V_HEADS])
        w_gate = w_gate.at[0, :, LANE:LANE + GDN_V_HEADS].set(w_tail[:, GDN_V_HEADS:])
        cw = gdn_conv_w[ia].reshape(GDN_CONV, ncols, LANE).transpose(1, 0, 2)
        cw = jnp.pad(cw, ((0, 0), (0, SUBLANE - GDN_CONV), (0, 0)))
        gp = jnp.zeros((SUBLANE, LANE), F32)
        gp = gp.at[0, :GDN_V_HEADS].set(gdn_a_log[ia]).at[1, :GDN_V_HEADS].set(gdn_dt_bias[ia])
        return w_gate, cw, gp, gdn_norm[ia]

    def run_trunk(x3, pos, prompt, mem_src):
        batch, seq_len, _ = x3.shape
        m = batch * seq_len
        x = x3.reshape(m, d_model)
        cos_f, sin_f = _rope_tables(pos)
        out = {"gdn_s": [], "gdn_c": [], "swa": [[] for _ in SWA_GROUPS], "ffn": [], "mem_kv": []}
        ia = ib = 0
        for layer in range(depth):
            g = norm_gains[layer]
            if layer % 2 == 0:
                w_gate, cw, gp, o_gain = gdn_weights(ia)
                p = norm_matmul(x, g[0], gdn_w_in_t, ia, n=main_dim, head_major=True, w_transposed=True)
                gates = norm_matmul(x, g[0], w_gate, 0)
                if prompt:
                    tail0 = jnp.zeros((batch, ncols, SUBLANE, LANE), F32)
                    s0 = jnp.zeros((batch, GDN_V_HEADS, LANE, LANE), F32)
                else:
                    cb = state_gdn_conv[ia].reshape(batch, GDN_CONV - 1, ncols, LANE).transpose(0, 2, 1, 3)
                    tail0 = jnp.pad(cb, ((0, 0), (0, 0), (SUBLANE - (GDN_CONV - 1), 0), (0, 0)))
                    s0 = state_gdn[ia]
                o_hm, s1 = gdn_core(p, gates, cw, gp, o_gain, tail0, s0, batch=batch, seq_len=seq_len)
                out["gdn_s"].append(s1)
                c1 = p.reshape(-1, batch, seq_len, LANE)[:ncols, :, seq_len - (GDN_CONV - 1):]
                out["gdn_c"].append(c1.transpose(1, 2, 0, 3).reshape(batch, GDN_CONV - 1, conv_dim))
                x = matmul_resnorm(o_hm, gdn_w_out16, ia, x, g[1], head_major=True)
                ia += 1
            else:
                ng = len(SWA_GROUPS)
                qkv_hm = norm_matmul(x, g[0], swa_w_qkv16, ib, head_major=True)
                if prompt:
                    o_hm, kr = swa_prompt(qkv_hm, cos_f, sin_f, batch=batch, seq_len=seq_len)
                    for gi, (window, _) in enumerate(SWA_GROUPS):
                        keep = min(window, seq_len)
                        kv = kv_pack(kr, qkv_hm, gi, batch=batch, seq_len=seq_len, keep=keep)
                        out["swa"][gi].append(kv.reshape(batch, keep, SWA_HPG, 2, LANE).transpose(0, 1, 3, 2, 4))
                else:
                    caches = [c[ib].transpose(0, 1, 3, 2, 4).reshape(batch, -1, seq_len * SWA_HPG * 2, LANE)
                              for c in (cache_swa_kv0, cache_swa_kv1, cache_swa_kv2)]
                    o_hm, rolled = swa_decode(qkv_hm, cos_f, sin_f, caches, batch=batch, steps=seq_len)
                    for gi in range(ng):
                        r5 = rolled[gi].reshape(batch, -1, SWA_HPG, 2, LANE)
                        out["swa"][gi].append(r5.transpose(0, 1, 3, 2, 4))
                x = matmul_resnorm(o_hm, swa_w_out16, ib, x, g[1], head_major=True)
                ib += 1
            q_hm = norm_matmul(x, g[2], mem_w_q, layer, head_major=True)
            if prompt:
                mem_len = mem_src.shape[1]
                kv_hm = norm_matmul(mem_src.reshape(batch * mem_len, d_model), g[2], mem_w_kv, layer,
                                    do_norm=False, head_major=True)
                out["mem_kv"].append(kv_hm.reshape(2, MEM_HEADS, batch, mem_len, LANE).transpose(2, 3, 0, 1, 4))
                o_hm = mem_prompt(q_hm, kv_hm, batch=batch, seq_len=seq_len, mem_len=mem_len, tq=1024)
            else:
                mem_rows = mem_src.reshape(mem_src.shape[0], batch, -1, LANE)
                o_hm = mem_decode(q_hm, mem_rows, layer, batch=batch, steps=seq_len)
            x = matmul_resnorm(o_hm, mem_w_o16, layer, x, g[3], head_major=True)
            if prompt:
                act, gt = ffn_up(x, g[4], ffn_w_up, layer, ffn_conv_w[layer], ffn_conv_b[layer], seq_len=seq_len)
                per_seq = gt.shape[0] // batch
                out["ffn"].append(gt[per_seq - 1::per_seq, SUBLANE - (FFN_CONV - 1):])
            else:
                buf = state_ffn_conv[layer]
                zero = jnp.zeros((batch, seq_len - 1, buf.shape[-1]), F32)
                hist1 = jnp.concatenate([buf[:, 1:2], zero], axis=1).reshape(m, -1)
                hist2 = jnp.concatenate([buf, zero[:, 1:]], axis=1).reshape(m, -1)
                act, gt = ffn_up(x, g[4], ffn_w_up, layer, ffn_conv_w[layer], ffn_conv_b[layer], seq_len=seq_len,
                                 hist=(hist1, hist2))
                out["ffn"].append(gt.reshape(batch, seq_len, -1)[:, seq_len - (FFN_CONV - 1):])
            x = matmul_resnorm(act, ffn_w_down16, layer, x, g[5], head_major=False)
        return x.reshape(batch, seq_len, d_model), out

    pos_p = jnp.arange(x_prompt.shape[1])
    y_p, op = run_trunk(x_prompt, pos_p, True, mem_prompt_in)
    pos_s = PAST_LEN + jnp.arange(x_sample.shape[1])
    y_s, os_ = run_trunk(x_sample, pos_s, False, cache_mem_kv)

    return (y_p, y_s, jnp.stack(op["gdn_s"]), jnp.stack(op["gdn_c"]),
            jnp.stack(op["swa"][0]), jnp.stack(op["swa"][1]), jnp.stack(op["swa"][2]),
            jnp.stack(op["mem_kv"]), jnp.stack(op["ffn"]),
            jnp.stack(os_["gdn_s"]), jnp.stack(os_["gdn_c"]),
            jnp.stack(os_["swa"][0]), jnp.stack(os_["swa"][1]), jnp.stack(os_["swa"][2]),
            jnp.stack(os_["ffn"]))
```
